```python
import math
import jax, jax.numpy as jnp
from jax import lax
import numpy as np

D_MODEL = 1024
BATCH = 8
SEQ = 2048
DEPTH = 2

SSD_HEADS = 12
SSD_HEAD_DIM = 64
SSD_INNER = SSD_HEADS * SSD_HEAD_DIM
SSD_GROUPS = 4
SSD_HPG = SSD_HEADS // SSD_GROUPS
SSD_STATE = 64
SSD_CONV = 4
SSD_CHUNK = 128
SSD_CONV_CH = SSD_INNER + 2 * SSD_GROUPS * SSD_STATE

S5_GROUP = 16
S5_WIDTH = 512
S5_GROUPS = S5_WIDTH // S5_GROUP
S5_STATE = 64

SC_WIDTH = 512
SC_CONV = 3

CF_WIDTH = 512
CF_CONV = 31

N_BRANCH = 4
BRANCH_OFFSETS = (0, SSD_INNER, SSD_INNER + S5_WIDTH, SSD_INNER + S5_WIDTH + SC_WIDTH,
                  SSD_INNER + S5_WIDTH + SC_WIDTH + CF_WIDTH)
MIX_WIDTH = BRANCH_OFFSETS[-1]

OFF_Z = 0
OFF_XBC = OFF_Z + SSD_INNER
OFF_DT = OFF_XBC + SSD_CONV_CH
OFF_S5 = OFF_DT + SSD_HEADS
OFF_SC = OFF_S5 + S5_WIDTH
OFF_CF = OFF_SC + 3 * SC_WIDTH
OFF_GATE = OFF_CF + 2 * CF_WIDTH
IN_COLS = OFF_GATE + N_BRANCH * D_MODEL

PEER_HEADS = 8
PEER_KEYS = 128
PEER_TOPK = 16
PEER_QDIM = 256
PEER_HALF = PEER_QDIM // 2
N_EXPERTS = PEER_KEYS * PEER_KEYS
PEER_CHUNK = 128

EPS = 1e-6

kernel_name = "hybrid_ssd_s5_conv_peer_block"


def rms_norm(x, g):
    xf = x.astype(jnp.float32)
    y = xf * lax.rsqrt(jnp.mean(xf * xf, axis=-1, keepdims=True) + EPS)
    return (y * g.astype(jnp.float32)).astype(x.dtype)


def layer_norm(x, g, b):
    xf = x.astype(jnp.float32)
    mu = jnp.mean(xf, axis=-1, keepdims=True)
    var = jnp.mean(jnp.square(xf - mu), axis=-1, keepdims=True)
    y = (xf - mu) * lax.rsqrt(var + EPS)
    return (y * g.astype(jnp.float32) + b.astype(jnp.float32)).astype(x.dtype)


def causal_dwconv(x, w):
    k, c = w.shape
    return lax.conv_general_dilated(
        x, w[:, None, :].astype(x.dtype), window_strides=(1,), padding=[(k - 1, 0)],
        dimension_numbers=("NWC", "WIO", "NWC"), feature_group_count=c)


def segsum(a):
    t = a.shape[-1]
    rep = jnp.broadcast_to(a[..., :, None], a.shape + (t,))
    rep = jnp.where(jnp.tril(jnp.ones((t, t), bool), -1), rep, 0.0)
    cs = jnp.cumsum(rep, axis=-2)
    return jnp.where(jnp.tril(jnp.ones((t, t), bool), 0), cs, -jnp.inf)


def ssd_scan(xh, dt, a, bm, cm):
    bsz, s, g, j, p = xh.shape
    n = bm.shape[-1]
    nc, ln = s // SSD_CHUNK, SSD_CHUNK
    xdt = (xh * dt[..., None]).reshape(bsz, nc, ln, g, j, p)
    adt = (dt * a).reshape(bsz, nc, ln, g, j).transpose(0, 3, 4, 1, 2)
    bc = bm.reshape(bsz, nc, ln, g, n)
    cc = cm.reshape(bsz, nc, ln, g, n)
    a_cum = jnp.cumsum(adt, axis=-1)
    decay_in = jnp.exp(segsum(adt))
    cb = jnp.einsum("bclgn,bcsgn->bcgls", cc, bc)
    y_diag = jnp.einsum("bcgls,bgjcls,bcsgjp->bclgjp", cb, decay_in, xdt)
    decay_states = jnp.exp(a_cum[..., -1:] - a_cum)
    states = jnp.einsum("bclgn,bgjcl,bclgjp->bcgjpn", bc, decay_states, xdt)
    states = jnp.concatenate([jnp.zeros_like(states[:, :1]), states], axis=1)
    chunk_tot = jnp.pad(a_cum[..., -1], ((0, 0), (0, 0), (0, 0), (1, 0)))
    decay_chunk = jnp.exp(segsum(chunk_tot))
    states = jnp.einsum("bgjzc,bcgjpn->bzgjpn", decay_chunk, states)[:, :-1]
    y_off = jnp.einsum("bclgn,bcgjpn,bgjcl->bclgjp", cc, states, jnp.exp(a_cum))
    return (y_diag + y_off).reshape(bsz, s, g, j, p)


def ssd_branch(proj, conv_w, conv_b, dt_bias, a_log, d_skip, norm_g):
    f32 = jnp.float32
    bsz, s, _ = proj.shape
    z = proj[..., OFF_Z:OFF_Z + SSD_INNER]
    xbc = proj[..., OFF_XBC:OFF_XBC + SSD_CONV_CH]
    dt_raw = proj[..., OFF_DT:OFF_DT + SSD_HEADS]
    xbc = jax.nn.silu(causal_dwconv(xbc, conv_w) + conv_b.astype(proj.dtype)).astype(f32)
    gn = SSD_GROUPS * SSD_STATE
    xs = xbc[..., :SSD_INNER].reshape(bsz, s, SSD_GROUPS, SSD_HPG, SSD_HEAD_DIM)
    bm = xbc[..., SSD_INNER:SSD_INNER + gn].reshape(bsz, s, SSD_GROUPS, SSD_STATE)
    cm = xbc[..., SSD_INNER + gn:].reshape(bsz, s, SSD_GROUPS, SSD_STATE)
    dt = jax.nn.softplus(dt_raw.astype(f32) + dt_bias.astype(f32)).reshape(bsz, s, SSD_GROUPS, SSD_HPG)
    a = -jnp.exp(a_log.astype(f32)).reshape(SSD_GROUPS, SSD_HPG)
    y = ssd_scan(xs, dt, a, bm, cm) + d_skip.astype(f32).reshape(SSD_GROUPS, SSD_HPG)[..., None] * xs
    y = y.reshape(bsz, s, SSD_INNER) * jax.nn.silu(z.astype(f32))
    return rms_norm(y, norm_g).astype(proj.dtype)


def _lin_rec_combine(e_i, e_j):
    a_i, b_i = e_i
    a_j, b_j = e_j
    return a_j * a_i, a_j * b_i + b_j


def s5_branch(u, lam_re, lam_im, log_step, b_re, b_im, c_re, c_im, d_skip, w_glu):
    f32 = jnp.float32
    bsz, s, _ = u.shape
    lam = lax.complex(lam_re.astype(f32), lam_im.astype(f32))
    step = jnp.exp(log_step.astype(f32))[:, None]
    lam_bar = jnp.exp(lam * step)
    b_bar = ((lam_bar - 1.0) / lam)[..., None] * lax.complex(b_re.astype(f32), b_im.astype(f32))
    c = lax.complex(c_re.astype(f32), c_im.astype(f32))
    ug = u.astype(f32).reshape(bsz, s, S5_GROUPS, S5_GROUP)
    bu = jnp.einsum("bsgi,gni->bsgn", ug.astype(jnp.complex64), b_bar)
    a = jnp.broadcast_to(lam_bar, (1, s) + lam_bar.shape)
    _, hs = lax.associative_scan(_lin_rec_combine, (a, bu), axis=1)
    y = jnp.einsum("bsgn,gon->bsgo", hs, c).real + d_skip.astype(f32).reshape(S5_GROUPS, S5_GROUP) * ug
    y = jax.nn.gelu(y.reshape(bsz, s, S5_WIDTH)).astype(u.dtype)
    return y * jax.nn.sigmoid(y @ w_glu)


def shortconv_branch(p_sc, conv_w):
    b, c, h = jnp.split(p_sc, 3, axis=-1)
    return b * causal_dwconv(c * h, conv_w)


def conformer_branch(p_cf, conv_w, ln_g, ln_b):
    a, g = jnp.split(p_cf, 2, axis=-1)
    y = causal_dwconv(a * jax.nn.sigmoid(g), conv_w)
    return jax.nn.silu(layer_norm(y, ln_g, ln_b))


def peer_ffn(h, w_query, sub_keys, expert_u, expert_v):
    f32 = jnp.float32
    bsz, s, d = h.shape
    t = bsz * s
    ht = h.reshape(t, d)
    q = (ht @ w_query).reshape(t, PEER_HEADS, 2, PEER_HALF).astype(f32)
    sc = jnp.einsum("thpd,pkd->thpk", q, sub_keys.astype(f32))
    s_top, i_top = lax.top_k(sc, PEER_TOPK)
    cand = (s_top[:, :, 0, :, None] + s_top[:, :, 1, None, :]).reshape(t, PEER_HEADS, PEER_TOPK * PEER_TOPK)
    best, pos = lax.top_k(cand, PEER_TOPK)
    idx = (jnp.take_along_axis(i_top[:, :, 0, :], pos // PEER_TOPK, axis=-1) * PEER_KEYS
           + jnp.take_along_axis(i_top[:, :, 1, :], pos % PEER_TOPK, axis=-1))
    gate = jax.nn.softmax(best, axis=-1).astype(h.dtype)

    def chunk_fn(args):
        hc, ic, gc = args
        act = jax.nn.gelu(jnp.einsum("chkd,cd->chk", expert_u[ic], hc)) * gc
        return jnp.einsum("chk,chkd->cd", act, expert_v[ic])

    nc = t // PEER_CHUNK
    out = lax.map(chunk_fn, (ht.reshape(nc, PEER_CHUNK, d),
                             idx.reshape(nc, PEER_CHUNK, PEER_HEADS, PEER_TOPK),
                             gate.reshape(nc, PEER_CHUNK, PEER_HEADS, PEER_TOPK)))
    return out.reshape(bsz, s, d)


def setup_inputs(seed: int = 0) -> dict:
    key = jax.random.key(seed)
    ks = jax.random.split(key, 32)
    f32 = jnp.float32
    L = DEPTH

    def nrm(k, shape, scale):
        return jax.random.normal(k, shape, f32) * scale

    x = nrm(ks[0], (BATCH, SEQ, D_MODEL), 1.0)
    norm1_g = 1.0 + nrm(ks[1], (L, D_MODEL), 0.02)
    w_in = nrm(ks[2], (L, D_MODEL, IN_COLS), D_MODEL ** -0.5)
    ssd_conv_w = nrm(ks[3], (L, SSD_CONV, SSD_CONV_CH), SSD_CONV ** -0.5)
    ssd_conv_b = nrm(ks[4], (L, SSD_CONV_CH), 0.01)
    dt0 = jnp.exp(jax.random.uniform(ks[5], (L, SSD_HEADS), f32, math.log(1e-3), math.log(1e-1)))
    ssd_dt_bias = dt0 + jnp.log(-jnp.expm1(-dt0))
    ssd_a_log = jnp.log(jax.random.uniform(ks[6], (L, SSD_HEADS), f32, 1.0, 16.0))
    ssd_d = 1.0 + nrm(ks[7], (L, SSD_HEADS), 0.02)
    ssd_norm_g = 1.0 + nrm(ks[8], (L, SSD_INNER), 0.02)
    n_idx = jnp.arange(S5_STATE, dtype=f32)
    s5_lam_re = -0.5 + nrm(ks[9], (L, S5_GROUPS, S5_STATE), 0.01)
    s5_lam_im = math.pi * n_idx + nrm(ks[10], (L, S5_GROUPS, S5_STATE), 0.01)
    s5_log_step = jax.random.uniform(ks[11], (L, S5_GROUPS), f32, math.log(1e-3), math.log(1e-1))
    s5_b_re = nrm(ks[12], (L, S5_GROUPS, S5_STATE, S5_GROUP), (2 * S5_GROUP) ** -0.5)
    s5_b_im = nrm(ks[13], (L, S5_GROUPS, S5_STATE, S5_GROUP), (2 * S5_GROUP) ** -0.5)
    s5_c_re = nrm(ks[14], (L, S5_GROUPS, S5_GROUP, S5_STATE), (2 * S5_STATE) ** -0.5)
    s5_c_im = nrm(ks[15], (L, S5_GROUPS, S5_GROUP, S5_STATE), (2 * S5_STATE) ** -0.5)
    s5_d = nrm(ks[16], (L, S5_WIDTH), 1.0)
    s5_w_glu = nrm(ks[17], (L, S5_WIDTH, S5_WIDTH), S5_WIDTH ** -0.5)
    sc_conv_w = nrm(ks[18], (L, SC_CONV, SC_WIDTH), SC_CONV ** -0.5)
    cf_conv_w = nrm(ks[19], (L, CF_CONV, CF_WIDTH), CF_CONV ** -0.5)
    cf_ln_g = 1.0 + nrm(ks[20], (L, CF_WIDTH), 0.02)
    cf_ln_b = nrm(ks[21], (L, CF_WIDTH), 0.02)
    row_scale = jnp.concatenate([jnp.full((BRANCH_OFFSETS[i + 1] - BRANCH_OFFSETS[i],),
                                          float(BRANCH_OFFSETS[i + 1] - BRANCH_OFFSETS[i]) ** -0.5, f32)
                                 for i in range(N_BRANCH)])
    w_branch = nrm(ks[22], (L, MIX_WIDTH, D_MODEL), 1.0) * row_scale[None, :, None]
    w_out = nrm(ks[23], (L, D_MODEL, D_MODEL), D_MODEL ** -0.5)
    norm2_g = 1.0 + nrm(ks[24], (L, D_MODEL), 0.02)
    peer_w_query = nrm(ks[25], (L, D_MODEL, PEER_HEADS * PEER_QDIM), D_MODEL ** -0.5)
    peer_sub_keys = nrm(ks[26], (L, 2, PEER_KEYS, PEER_HALF), PEER_HALF ** -0.5)
    peer_u = nrm(ks[27], (L, N_EXPERTS, D_MODEL), D_MODEL ** -0.5)
    peer_v = nrm(ks[28], (L, N_EXPERTS, D_MODEL), (PEER_HEADS * PEER_TOPK) ** -0.5)
    final_norm_g = 1.0 + nrm(ks[29], (D_MODEL,), 0.02)
    return {
        "x": x, "norm1_g": norm1_g, "w_in": w_in,
        "ssd_conv_w": ssd_conv_w, "ssd_conv_b": ssd_conv_b, "ssd_dt_bias": ssd_dt_bias,
        "ssd_a_log": ssd_a_log, "ssd_d": ssd_d, "ssd_norm_g": ssd_norm_g,
        "s5_lam_re": s5_lam_re, "s5_lam_im": s5_lam_im, "s5_log_step": s5_log_step,
        "s5_b_re": s5_b_re, "s5_b_im": s5_b_im, "s5_c_re": s5_c_re, "s5_c_im": s5_c_im,
        "s5_d": s5_d, "s5_w_glu": s5_w_glu,
        "sc_conv_w": sc_conv_w,
        "cf_conv_w": cf_conv_w, "cf_ln_g": cf_ln_g, "cf_ln_b": cf_ln_b,
        "w_branch": w_branch, "w_out": w_out, "norm2_g": norm2_g,
        "peer_w_query": peer_w_query, "peer_sub_keys": peer_sub_keys,
        "peer_u": peer_u, "peer_v": peer_v, "final_norm_g": final_norm_g,
    }


def reference(x, norm1_g, w_in, ssd_conv_w, ssd_conv_b, ssd_dt_bias, ssd_a_log, ssd_d, ssd_norm_g,
              s5_lam_re, s5_lam_im, s5_log_step, s5_b_re, s5_b_im, s5_c_re, s5_c_im, s5_d, s5_w_glu,
              sc_conv_w, cf_conv_w, cf_ln_g, cf_ln_b, w_branch, w_out, norm2_g,
              peer_w_query, peer_sub_keys, peer_u, peer_v, final_norm_g):
    bsz, s, d = x.shape
    for l in range(DEPTH):
        h = rms_norm(x, norm1_g[l])
        proj = h @ w_in[l]
        y_a = ssd_branch(proj, ssd_conv_w[l], ssd_conv_b[l], ssd_dt_bias[l], ssd_a_log[l], ssd_d[l], ssd_norm_g[l])
        y_b = s5_branch(proj[..., OFF_S5:OFF_S5 + S5_WIDTH], s5_lam_re[l], s5_lam_im[l], s5_log_step[l],
                        s5_b_re[l], s5_b_im[l], s5_c_re[l], s5_c_im[l], s5_d[l], s5_w_glu[l])
        y_c = shortconv_branch(proj[..., OFF_SC:OFF_SC + 3 * SC_WIDTH], sc_conv_w[l])
        y_d = conformer_branch(proj[..., OFF_CF:OFF_CF + 2 * CF_WIDTH], cf_conv_w[l], cf_ln_g[l], cf_ln_b[l])
        ys = (y_a, y_b, y_c, y_d)
        gates = jax.nn.sigmoid(proj[..., OFF_GATE:]).reshape(bsz, s, N_BRANCH, d)
        merged = gates[:, :, 0, :] * (ys[0] @ w_branch[l, BRANCH_OFFSETS[0]:BRANCH_OFFSETS[1]])
        for i in range(1, N_BRANCH):
            merged = merged + gates[:, :, i, :] * (ys[i] @ w_branch[l, BRANCH_OFFSETS[i]:BRANCH_OFFSETS[i + 1]])
        x = x + merged @ w_out[l]
        x = x + peer_ffn(rms_norm(x, norm2_g[l]), peer_w_query[l], peer_sub_keys[l], peer_u[l], peer_v[l])
    return rms_norm(x, final_norm_g)
```

```python
import functools

import jax
import jax.numpy as jnp
from jax import lax
from jax.experimental import pallas as pl
from jax.experimental.pallas import tpu as pltpu

F32 = jnp.float32
BF16 = jnp.bfloat16
EPS = 1e-6

D_MODEL = 1024
SSD_HEADS = 12
SSD_HEAD_DIM = 64
SSD_INNER = SSD_HEADS * SSD_HEAD_DIM
SSD_GROUPS = 4
SSD_HPG = SSD_HEADS // SSD_GROUPS
SSD_STATE = 64
SSD_CONV = 4
SSD_CHUNK = 128
SSD_CONV_CH = SSD_INNER + 2 * SSD_GROUPS * SSD_STATE
S5_GROUP = 16
S5_WIDTH = 512
S5_GROUPS = S5_WIDTH // S5_GROUP
S5_STATE = 64
S5_LANES = S5_GROUPS * S5_STATE
SC_WIDTH = 512
SC_CONV = 3
CF_WIDTH = 512
CF_CONV = 31
N_BRANCH = 4
BRANCH_OFFSETS = (0, 768, 1280, 1792, 2304)
PEER_HEADS = 8
PEER_KEYS = 128
PEER_TOPK = 16
PEER_HALF = 128
PEER_SEL = PEER_HEADS * PEER_TOPK
N_EXPERTS = PEER_KEYS * PEER_KEYS

OFF_Z = 0
OFF_XBC = OFF_Z + SSD_INNER
OFF_DT = OFF_XBC + SSD_CONV_CH
OFF_S5 = OFF_DT + SSD_HEADS
OFF_SC = OFF_S5 + S5_WIDTH
OFF_CF = OFF_SC + 3 * SC_WIDTH
OFF_GATE = OFF_CF + 2 * CF_WIDTH
IN_COLS = OFF_GATE + N_BRANCH * D_MODEL

LANE = 128
P_ZX = 0
P_S5 = 2048
P_SC = P_S5 + S5_WIDTH
P_CF = P_SC + 3 * SC_WIDTH
P_GATE = P_CF + 2 * CF_WIDTH
P_DT = P_GATE + N_BRANCH * D_MODEL
P_COLS = 9728
PROJ_TN = 512

VMEM_LIMIT = 56 * 1024 * 1024


def _cparams(*sem):
    return pltpu.CompilerParams(dimension_semantics=sem, vmem_limit_bytes=VMEM_LIMIT)


def _dot(a, b):
    return jnp.dot(a, b, preferred_element_type=F32)


def _dot_nt(a, b):
    return lax.dot_general(a, b, (((1,), (1,)), ((), ())), preferred_element_type=F32)


def _rms(x, g):
    return x * lax.rsqrt(jnp.mean(x * x, axis=-1, keepdims=True) + EPS) * g


def _inproj_kernel(x_ref, g_ref, w_ref, o_ref, h_ref):
    @pl.when(pl.program_id(1) == 0)
    def _():
        h_ref[...] = _rms(x_ref[...], g_ref[...]).astype(BF16)

    o_ref[...] = _dot(h_ref[...], w_ref[...])


def _inproj(x, g, w):
    t = x.shape[0]
    tm = min(1024, t)
    return pl.pallas_call(
        _inproj_kernel,
        grid=(t // tm, P_COLS // PROJ_TN),
        in_specs=[pl.BlockSpec((tm, D_MODEL), lambda i, j: (i, 0)),
                  pl.BlockSpec((1, D_MODEL), lambda i, j: (0, 0)),
                  pl.BlockSpec((D_MODEL, PROJ_TN), lambda i, j: (0, j))],
        out_specs=pl.BlockSpec((tm, PROJ_TN), lambda i, j: (i, j)),
        out_shape=jax.ShapeDtypeStruct((t, P_COLS), F32),
        scratch_shapes=[pltpu.VMEM((tm, D_MODEL), BF16)],
        compiler_params=_cparams("arbitrary", "arbitrary"),
        name="inproj",
    )(x, g, w)


def _split3(x):
    hi = x.astype(BF16)
    r = x - hi.astype(F32)
    mid = r.astype(BF16)
    lo = (r - mid.astype(F32)).astype(BF16)
    return hi, mid, lo


def _ssd_kernel(zx_ref, dt_ref, cw_ref, cb_ref, dtb_ref, alog_ref, dsk_ref, ng_ref, o_ref,
                xbuf, state, ybuf):
    L = SSD_CHUNK
    c = pl.program_id(1)

    @pl.when(c == 0)
    def _():
        xbuf[0:8, :] = jnp.zeros((8, SSD_CONV_CH), F32)
        state[...] = jnp.zeros(state.shape, F32)

    xbuf[8:8 + L, :] = zx_ref[:, SSD_INNER:SSD_INNER + SSD_CONV_CH]
    conv = cb_ref[...] + cw_ref[0:1, :] * xbuf[5:5 + L, :]
    for k in range(1, SSD_CONV):
        conv = conv + cw_ref[k:k + 1, :] * xbuf[5 + k:5 + k + L, :]
    xbuf[0:8, :] = xbuf[L:L + 8, :]
    xc = jax.nn.silu(conv)

    gn = SSD_GROUPS * SSD_STATE
    bm = xc[:, SSD_INNER:SSD_INNER + gn]
    cm = xc[:, SSD_INNER + gn:SSD_INNER + 2 * gn]
    bm_b = bm.astype(BF16)
    cm_b = cm.astype(BF16)
    bmT_b = bm.T.astype(BF16)

    dt = jax.nn.softplus(dt_ref[...] + dtb_ref[...])
    a = -jnp.exp(alog_ref[...])
    adt = dt * a

    row = lax.broadcasted_iota(jnp.int32, (L, L), 0)
    col = lax.broadcasted_iota(jnp.int32, (L, L), 1)
    tril = row >= col
    tri_b = jnp.where(tril, 1.0, 0.0).astype(BF16)
    hi, mid, lo = _split3(adt)
    a_cum = _dot(tri_b, hi) + _dot(tri_b, mid) + _dot(tri_b, lo)
    a_cum_t = a_cum.T
    a_last = a_cum[L - 1:L, :]

    for g in range(SSD_GROUPS):
        cm_g = cm_b[:, g * SSD_STATE:(g + 1) * SSD_STATE]
        bm_g = bm_b[:, g * SSD_STATE:(g + 1) * SSD_STATE]
        bmT_g = bmT_b[g * SSD_STATE:(g + 1) * SSD_STATE, :]
        cbm = _dot_nt(cm_g, bm_g)
        for j in range(SSD_HPG):
            h = g * SSD_HPG + j
            xs_h = xc[:, h * SSD_HEAD_DIM:(h + 1) * SSD_HEAD_DIM]
            dt_h = dt[:, h:h + 1]
            ac_h = a_cum[:, h:h + 1]
            xdt_h = xs_h * dt_h
            dec = jnp.exp(jnp.where(tril, ac_h - a_cum_t[h:h + 1, :], -jnp.inf))
            y = _dot((cbm * dec).astype(BF16), xdt_h.astype(BF16))
            prev = state[h]
            y = y + _dot(cm_g, prev.astype(BF16)) * jnp.exp(ac_h)
            al_h = a_last[:, h:h + 1]
            w = (xdt_h * jnp.exp(al_h - ac_h)).astype(BF16)
            state[h] = jnp.exp(al_h) * prev + _dot(bmT_g, w)
            ybuf[:, h * SSD_HEAD_DIM:(h + 1) * SSD_HEAD_DIM] = y + dsk_ref[:, h:h + 1] * xs_h

    z = zx_ref[:, 0:SSD_INNER]
    o_ref[...] = _rms(ybuf[...] * jax.nn.silu(z), ng_ref[...])


def _ssd(proj, bsz, s, cw, cb, dtb, alog, dsk, ng):
    t = bsz * s
    nc = s // SSD_CHUNK
    L = SSD_CHUNK
    const = lambda b, c: (0, 0)
    return pl.pallas_call(
        _ssd_kernel,
        grid=(bsz, nc),
        in_specs=[pl.BlockSpec((L, 2048), lambda b, c: (b * nc + c, 0)),
                  pl.BlockSpec((L, LANE), lambda b, c: (b * nc + c, P_DT // LANE)),
                  pl.BlockSpec((SSD_CONV, SSD_CONV_CH), const),
                  pl.BlockSpec((1, SSD_CONV_CH), const),
                  pl.BlockSpec((1, LANE), const),
                  pl.BlockSpec((1, LANE), const),
                  pl.BlockSpec((1, LANE), const),
                  pl.BlockSpec((1, SSD_INNER), const)],
        out_specs=pl.BlockSpec((L, SSD_INNER), lambda b, c: (b * nc + c, 0)),
        out_shape=jax.ShapeDtypeStruct((t, SSD_INNER), F32),
        scratch_shapes=[pltpu.VMEM((L + 8, SSD_CONV_CH), F32),
                        pltpu.VMEM((SSD_HEADS, SSD_STATE, SSD_HEAD_DIM), F32),
                        pltpu.VMEM((L, SSD_INNER), F32)],
        compiler_params=_cparams("arbitrary", "arbitrary"),
        name="ssd",
    )(proj, proj, cw, cb, dtb, alog, dsk, ng)


def _s5_disc_kernel(lre_ref, lim_ref, ls_ref, bre_ref, bim_ref, ore_ref, oim_ref, obre_ref, obim_ref):
    lr = lre_ref[...]
    li = lim_ref[...]
    step = jnp.exp(ls_ref[...])
    mag = jnp.exp(lr * step)
    pr = mag * jnp.cos(li * step)
    pi = mag * jnp.sin(li * step)
    ore_ref[...] = pr
    oim_ref[...] = pi
    qr = pr - 1.0
    den = lr * lr + li * li
    cr = (qr * lr + pi * li) / den
    ci = (pi * lr - qr * li) / den
    br = bre_ref[...]
    bi = bim_ref[...]
    obre_ref[...] = cr * br - ci * bi
    obim_ref[...] = cr * bi + ci * br


def _s5_discretise(lam_re, lam_im, log_step, b_re, b_im):
    n = S5_STATE * S5_GROUP
    rep = lambda a: jnp.repeat(a, S5_GROUP, axis=-1)
    args = (rep(lam_re), rep(lam_im), jnp.broadcast_to(log_step[:, None], (S5_GROUPS, n)),
            b_re.reshape(S5_GROUPS, n), b_im.reshape(S5_GROUPS, n))
    outs = pl.pallas_call(
        _s5_disc_kernel,
        out_shape=[jax.ShapeDtypeStruct((S5_GROUPS, n), F32)] * 4,
        name="s5_disc",
    )(*args)
    lamb_re, lamb_im, bb_re, bb_im = outs
    lamb_re = lamb_re[:, ::S5_GROUP].reshape(1, S5_LANES)
    lamb_im = lamb_im[:, ::S5_GROUP].reshape(1, S5_LANES)
    return lamb_re, lamb_im, bb_re.reshape(S5_GROUPS, S5_STATE, S5_GROUP), bb_im.reshape(S5_GROUPS, S5_STATE, S5_GROUP)


def _block_diag(blocks):
    g, r, c = blocks.shape
    eye = jnp.eye(g, dtype=blocks.dtype)
    return (blocks[:, :, None, :] * eye[:, None, :, None]).reshape(g * r, g * c)


S5_COLS = 512


def _s5_kernel(u_ref, lre_ref, lim_ref, bre_ref, bim_ref, cre_ref, cimn_ref, d_ref, wglu_ref, o_ref,
               hre, him, sre, sim):
    rows = u_ref.shape[0]
    steps = rows // 8

    @pl.when(pl.program_id(0) == 0)
    def _():
        sre[...] = jnp.zeros(sre.shape, F32)
        sim[...] = jnp.zeros(sim.shape, F32)

    u = u_ref[...]
    ub = u.astype(BF16)
    hre[...] = _dot(ub, bre_ref[...])
    him[...] = _dot(ub, bim_ref[...])

    for cb in range(S5_LANES // S5_COLS):
        cols = slice(cb * S5_COLS, (cb + 1) * S5_COLS)
        lr = jnp.broadcast_to(lre_ref[:, cols], (8, S5_COLS))
        li = jnp.broadcast_to(lim_ref[:, cols], (8, S5_COLS))

        def body(t, carry, cols=cols, lr=lr, li=li):
            hr, hi = carry
            r0 = pl.multiple_of(t * 8, 8)
            nr = lr * hr - li * hi + hre[pl.ds(r0, 8), cols]
            ni = lr * hi + li * hr + him[pl.ds(r0, 8), cols]
            hre[pl.ds(r0, 8), cols] = nr
            him[pl.ds(r0, 8), cols] = ni
            return nr, ni

        hr, hi = lax.fori_loop(0, steps, body, (sre[:, cols], sim[:, cols]), unroll=8)
        sre[:, cols] = hr
        sim[:, cols] = hi

    y = _dot(hre[...].astype(BF16), cre_ref[...]) + _dot(him[...].astype(BF16), cimn_ref[...]) + d_ref[...] * u
    y = jax.nn.gelu(y)
    o_ref[...] = y * jax.nn.sigmoid(_dot(y.astype(BF16), wglu_ref[...]))


def _s5(u_sb, lamb_re, lamb_im, bmat_re, bmat_im, cmat_re, cmat_imn, d, wglu):
    rows_total = u_sb.shape[0]
    rows = min(512, rows_total)
    const = lambda i: (0, 0)
    return pl.pallas_call(
        _s5_kernel,
        grid=(rows_total // rows,),
        in_specs=[pl.BlockSpec((rows, S5_WIDTH), lambda i: (i, 0)),
                  pl.BlockSpec((1, S5_LANES), const),
                  pl.BlockSpec((1, S5_LANES), const),
                  pl.BlockSpec((S5_WIDTH, S5_LANES), const),
                  pl.BlockSpec((S5_WIDTH, S5_LANES), const),
                  pl.BlockSpec((S5_LANES, S5_WIDTH), const),
                  pl.BlockSpec((S5_LANES, S5_WIDTH), const),
                  pl.BlockSpec((1, S5_WIDTH), const),
                  pl.BlockSpec((S5_WIDTH, S5_WIDTH), const)],
        out_specs=pl.BlockSpec((rows, S5_WIDTH), lambda i: (i, 0)),
        out_shape=jax.ShapeDtypeStruct((rows_total, S5_WIDTH), F32),
        scratch_shapes=[pltpu.VMEM((rows, S5_LANES), F32), pltpu.VMEM((rows, S5_LANES), F32),
                        pltpu.VMEM((8, S5_LANES), F32), pltpu.VMEM((8, S5_LANES), F32)],
        compiler_params=_cparams("arbitrary"),
        name="s5",
    )(u_sb, lamb_re, lamb_im, bmat_re, bmat_im, cmat_re, cmat_imn, d, wglu)


CONV_ROWS = 64
SC_HALO = 8
CF_HALO = 32


def _conv_kernel(scb_ref, scc_ref, sch_ref, cf_ref, scw_ref, cfw_ref, lng_ref, lnb_ref, oc_ref, od_ref, chbuf, vbuf):
    L = scb_ref.shape[0]
    c = pl.program_id(1)

    @pl.when(c == 0)
    def _():
        chbuf[0:SC_HALO, :] = jnp.zeros((SC_HALO, SC_WIDTH), F32)
        vbuf[0:CF_HALO, :] = jnp.zeros((CF_HALO, CF_WIDTH), F32)

    chbuf[SC_HALO:SC_HALO + L, :] = scc_ref[...] * sch_ref[...]
    a = cf_ref[:, 0:CF_WIDTH]
    vbuf[CF_HALO:CF_HALO + L, :] = a * jax.nn.sigmoid(cf_ref[:, CF_WIDTH:2 * CF_WIDTH])

    for r0 in range(0, L, CONV_ROWS):
        base = SC_HALO - (SC_CONV - 1) + r0
        acc = scw_ref[0:1, :] * chbuf[base:base + CONV_ROWS, :]
        for k in range(1, SC_CONV):
            acc = acc + scw_ref[k:k + 1, :] * chbuf[base + k:base + k + CONV_ROWS, :]
        oc_ref[r0:r0 + CONV_ROWS, :] = scb_ref[r0:r0 + CONV_ROWS, :] * acc

        base = CF_HALO - (CF_CONV - 1) + r0
        acc = cfw_ref[0:1, :] * vbuf[base:base + CONV_ROWS, :]
        for k in range(1, CF_CONV):
            acc = acc + cfw_ref[k:k + 1, :] * vbuf[base + k:base + k + CONV_ROWS, :]
        mu = jnp.mean(acc, axis=-1, keepdims=True)
        var = jnp.mean(jnp.square(acc - mu), axis=-1, keepdims=True)
        yn = (acc - mu) * lax.rsqrt(var + EPS) * lng_ref[...] + lnb_ref[...]
        od_ref[r0:r0 + CONV_ROWS, :] = jax.nn.silu(yn)

    chbuf[0:SC_HALO, :] = chbuf[L:L + SC_HALO, :]
    vbuf[0:CF_HALO, :] = vbuf[L:L + CF_HALO, :]


def _convs(proj, bsz, s, scw, cfw, lng, lnb):
    t = bsz * s
    L = min(256, s)
    nc = s // L
    const = lambda b, c: (0, 0)
    return pl.pallas_call(
        _conv_kernel,
        grid=(bsz, nc),
        in_specs=[pl.BlockSpec((L, SC_WIDTH), lambda b, c: (b * nc + c, P_SC // SC_WIDTH)),
                  pl.BlockSpec((L, SC_WIDTH), lambda b, c: (b * nc + c, P_SC // SC_WIDTH + 1)),
                  pl.BlockSpec((L, SC_WIDTH), lambda b, c: (b * nc + c, P_SC // SC_WIDTH + 2)),
                  pl.BlockSpec((L, 2 * CF_WIDTH), lambda b, c: (b * nc + c, P_CF // (2 * CF_WIDTH))),
                  pl.BlockSpec((SC_CONV, SC_WIDTH), const),
                  pl.BlockSpec((CF_CONV, CF_WIDTH), const),
                  pl.BlockSpec((1, CF_WIDTH), const),
                  pl.BlockSpec((1, CF_WIDTH), const)],
        out_specs=[pl.BlockSpec((L, SC_WIDTH), lambda b, c: (b * nc + c, 0)),
                   pl.BlockSpec((L, CF_WIDTH), lambda b, c: (b * nc + c, 0))],
        out_shape=[jax.ShapeDtypeStruct((t, SC_WIDTH), F32), jax.ShapeDtypeStruct((t, CF_WIDTH), F32)],
        scratch_shapes=[pltpu.VMEM((L + SC_HALO, SC_WIDTH), F32), pltpu.VMEM((L + CF_HALO, CF_WIDTH), F32)],
        compiler_params=_cparams("arbitrary", "arbitrary"),
        name="convs",
    )(proj, proj, proj, proj, scw, cfw, lng, lnb)


def _merge_kernel(x_ref, ya_ref, yb_ref, yc_ref, yd_ref, ga_ref, gb_ref, gc_ref, gd_ref, wb_ref, wout_ref, g2_ref,
                  wq_ref, xo_ref, h2_ref, q_ref):
    merged = None
    branches = ((ya_ref, ga_ref), (yb_ref, gb_ref), (yc_ref, gc_ref), (yd_ref, gd_ref))
    for i, (y_ref, gate_ref) in enumerate(branches):
        lo, hi = BRANCH_OFFSETS[i], BRANCH_OFFSETS[i + 1]
        t = _dot(y_ref[...].astype(BF16), wb_ref[lo:hi, :])
        t = jax.nn.sigmoid(gate_ref[...]) * t
        merged = t if merged is None else merged + t
    xn = x_ref[...] + _dot(merged.astype(BF16), wout_ref[...])
    xo_ref[...] = xn
    h2 = _rms(xn, g2_ref[...]).astype(BF16)
    h2_ref[...] = h2
    q_ref[...] = _dot(h2, wq_ref[...]).astype(BF16)


def _merge(x, ya, yb, yc, yd, proj, wb, wout, g2, wq):
    t = x.shape[0]
    tm = min(512, t)
    nq = wq.shape[1]
    row = lambda i: (i, 0)
    const = lambda i: (0, 0)
    return pl.pallas_call(
        _merge_kernel,
        grid=(t // tm,),
        in_specs=[pl.BlockSpec((tm, D_MODEL), row),
                  pl.BlockSpec((tm, SSD_INNER), row),
                  pl.BlockSpec((tm, S5_WIDTH), row),
                  pl.BlockSpec((tm, SC_WIDTH), row),
                  pl.BlockSpec((tm, CF_WIDTH), row),
                  pl.BlockSpec((tm, D_MODEL), lambda i: (i, P_GATE // D_MODEL)),
                  pl.BlockSpec((tm, D_MODEL), lambda i: (i, P_GATE // D_MODEL + 1)),
                  pl.BlockSpec((tm, D_MODEL), lambda i: (i, P_GATE // D_MODEL + 2)),
                  pl.BlockSpec((tm, D_MODEL), lambda i: (i, P_GATE // D_MODEL + 3)),
                  pl.BlockSpec((BRANCH_OFFSETS[-1], D_MODEL), const),
                  pl.BlockSpec((D_MODEL, D_MODEL), const),
                  pl.BlockSpec((1, D_MODEL), const),
                  pl.BlockSpec((D_MODEL, nq), const)],
        out_specs=[pl.BlockSpec((tm, D_MODEL), row), pl.BlockSpec((tm, D_MODEL), row), pl.BlockSpec((tm, nq), row)],
        out_shape=[jax.ShapeDtypeStruct((t, D_MODEL), F32), jax.ShapeDtypeStruct((t, D_MODEL), BF16),
                   jax.ShapeDtypeStruct((t, nq), BF16)],
        compiler_params=_cparams("arbitrary"),
        name="merge",
    )(x, ya, yb, yc, yd, proj, proj, proj, proj, wb, wout, g2, wq)


NEG_INF = float("-inf")
CAND_ROWS = 2 * PEER_TOPK + (PEER_TOPK - 2) * 8


def _top16(x, vals_ref, idx_ref, ids):
    big = jnp.float32(1e9)
    for r in range(PEER_TOPK):
        m = jnp.max(x, axis=0, keepdims=True)
        sel = jnp.min(jnp.where(x == m, ids, big), axis=0, keepdims=True)
        vals_ref[r:r + 1, :] = m
        idx_ref[r:r + 1, :] = sel
        x = jnp.where(ids == sel, NEG_INF, x)


def _route_kernel(q_ref, k1_ref, k2_ref, oi_ref, oj_ref, og_ref,
                  v1, i1, v2, i2, cand, best, pos, si, sj, sg):
    tn = q_ref.shape[0]
    key_ids = lax.broadcasted_iota(jnp.int32, (PEER_KEYS, tn), 0).astype(F32)
    r = lax.broadcasted_iota(jnp.int32, (CAND_ROWS, tn), 0)
    pos_ids = jnp.where(r < 2 * PEER_TOPK, r,
                        (2 + ((r - 2 * PEER_TOPK) >> 3)) * PEER_TOPK + ((r - 2 * PEER_TOPK) & 7)).astype(F32)

    def head(h, carry):
        c0 = pl.multiple_of(h * 2 * PEER_HALF, 2 * PEER_HALF)
        q1 = q_ref[:, pl.ds(c0, PEER_HALF)]
        q2 = q_ref[:, pl.ds(c0 + PEER_HALF, PEER_HALF)]
        _top16(_dot_nt(k1_ref[...], q1), v1, i1, key_ids)
        _top16(_dot_nt(k2_ref[...], q2), v2, i2, key_ids)
        v2a = v2[...]
        cand[0:PEER_TOPK, :] = v1[0:1, :] + v2a
        cand[PEER_TOPK:2 * PEER_TOPK, :] = v1[1:2, :] + v2a
        for a in range(2, PEER_TOPK):
            o = 2 * PEER_TOPK + (a - 2) * 8
            cand[o:o + 8, :] = v1[a:a + 1, :] + v2a[0:8, :]
        _top16(cand[...], best, pos, pos_ids)
        p = pos[...]
        a_k = jnp.floor(p * (1.0 / PEER_TOPK))
        b_k = p - a_k * PEER_TOPK
        isel = jnp.zeros((PEER_TOPK, tn), F32)
        jsel = jnp.zeros((PEER_TOPK, tn), F32)
        for a in range(PEER_TOPK):
            isel = jnp.where(a_k == a, i1[a:a + 1, :], isel)
            jsel = jnp.where(b_k == a, i2[a:a + 1, :], jsel)
        r0 = pl.multiple_of(h * PEER_TOPK, PEER_TOPK)
        si[pl.ds(r0, PEER_TOPK), :] = isel
        sj[pl.ds(r0, PEER_TOPK), :] = jsel
        sg[pl.ds(r0, PEER_TOPK), :] = jax.nn.softmax(best[...], axis=0)
        return carry

    lax.fori_loop(0, PEER_HEADS, head, 0)
    oi_ref[...] = si[...].T
    oj_ref[...] = sj[...].T
    og_ref[...] = sg[...].T


def _route(q, k1, k2):
    t = q.shape[0]
    tn = min(256, t)
    row = lambda i: (i, 0)
    const = lambda i: (0, 0)
    vm = lambda n: pltpu.VMEM((n, tn), F32)
    return pl.pallas_call(
        _route_kernel,
        grid=(t // tn,),
        in_specs=[pl.BlockSpec((tn, q.shape[1]), row),
                  pl.BlockSpec((PEER_KEYS, PEER_HALF), const),
                  pl.BlockSpec((PEER_KEYS, PEER_HALF), const)],
        out_specs=[pl.BlockSpec((tn, PEER_SEL), row)] * 3,
        out_shape=[jax.ShapeDtypeStruct((t, PEER_SEL), F32)] * 3,
        scratch_shapes=[vm(PEER_TOPK), vm(PEER_TOPK), vm(PEER_TOPK), vm(PEER_TOPK), vm(CAND_ROWS),
                        vm(PEER_TOPK), vm(PEER_TOPK), vm(PEER_SEL), vm(PEER_SEL), vm(PEER_SEL)],
        compiler_params=_cparams("arbitrary"),
        name="route",
    )(q, k1, k2)


PEER_TE = 512
GS_PITCH = 136
GS_TOK = 16


def _peer_kernel(h_ref, ii_ref, jj_ref, gg_ref, x_ref, u_ref, v_ref, fg_ref, o_ref, gmat, acc, gs, *, final):
    tm = h_ref.shape[0]
    e = pl.program_id(1)

    @pl.when(e == 0)
    def _():
        acc[...] = jnp.zeros(acc.shape, F32)
        ids = lax.broadcasted_iota(jnp.int32, (PEER_KEYS, PEER_SEL), 0).astype(F32)

        def group(gi, carry):
            t0 = pl.multiple_of(gi * GS_TOK, GS_TOK)
            for tt in range(GS_TOK):
                irow = ii_ref[pl.ds(t0 + tt, 1), :]
                jrow = jj_ref[pl.ds(t0 + tt, 1), :]
                grow = gg_ref[pl.ds(t0 + tt, 1), :]
                pt = jnp.where(ids == irow, 1.0, 0.0).astype(BF16)
                qt = jnp.where(ids == jrow, grow, 0.0).astype(BF16)
                gs[tt * GS_PITCH:tt * GS_PITCH + PEER_KEYS, :] = _dot_nt(pt, qt)

            def col(i, c2):
                blk = gs[pl.ds(i, GS_TOK, stride=GS_PITCH), :]
                gmat[pl.ds(t0, GS_TOK), pl.ds(pl.multiple_of(i * PEER_KEYS, PEER_KEYS), PEER_KEYS)] = blk.astype(BF16)
                return c2

            lax.fori_loop(0, PEER_KEYS, col, 0, unroll=8)
            return carry

        lax.fori_loop(0, tm // GS_TOK, group, 0)

    s = _dot_nt(h_ref[...], u_ref[...])
    w = gmat[:, pl.ds(pl.multiple_of(e * PEER_TE, PEER_TE), PEER_TE)]
    a = (jax.nn.gelu(s) * w.astype(F32)).astype(BF16)
    acc[...] += _dot(a, v_ref[...])

    @pl.when(e == pl.num_programs(1) - 1)
    def _():
        xn = x_ref[...] + acc[...]
        o_ref[...] = _rms(xn, fg_ref[...]) if final else xn


def _peer(h2, ii, jj, gg, x, u, v, fg, final):
    t = h2.shape[0]
    tm = min(512, t)
    row = lambda i, e: (i, 0)
    return pl.pallas_call(
        functools.partial(_peer_kernel, final=final),
        grid=(t // tm, N_EXPERTS // PEER_TE),
        in_specs=[pl.BlockSpec((tm, D_MODEL), row),
                  pl.BlockSpec((tm, PEER_SEL), row),
                  pl.BlockSpec((tm, PEER_SEL), row),
                  pl.BlockSpec((tm, PEER_SEL), row),
                  pl.BlockSpec((tm, D_MODEL), row),
                  pl.BlockSpec((PEER_TE, D_MODEL), lambda i, e: (e, 0)),
                  pl.BlockSpec((PEER_TE, D_MODEL), lambda i, e: (e, 0)),
                  pl.BlockSpec((1, D_MODEL), lambda i, e: (0, 0))],
        out_specs=pl.BlockSpec((tm, D_MODEL), row),
        out_shape=jax.ShapeDtypeStruct((t, D_MODEL), F32),
        scratch_shapes=[pltpu.VMEM((tm, N_EXPERTS), BF16),
                        pltpu.VMEM((tm, D_MODEL), F32),
                        pltpu.VMEM((GS_TOK * GS_PITCH, PEER_KEYS), F32)],
        compiler_params=_cparams("arbitrary", "arbitrary"),
        name="peer",
    )(h2, ii, jj, gg, x, u, v, fg)


def _pad_lanes(v, n=LANE):
    return jnp.pad(v, (0, n - v.shape[0])).reshape(1, n)


def _relayout_w_in(w):
    seg = lambda o, n: w[:, o:o + n]
    parts = [seg(OFF_Z, SSD_INNER + SSD_CONV_CH), seg(OFF_S5, S5_WIDTH), seg(OFF_SC, 3 * SC_WIDTH),
             seg(OFF_CF, 2 * CF_WIDTH), seg(OFF_GATE, N_BRANCH * D_MODEL), seg(OFF_DT, SSD_HEADS)]
    wp = jnp.concatenate(parts, axis=1)
    return jnp.pad(wp, ((0, 0), (0, P_COLS - wp.shape[1]))).astype(BF16)


def kernel(x, norm1_g, w_in, ssd_conv_w, ssd_conv_b, ssd_dt_bias, ssd_a_log, ssd_d, ssd_norm_g, s5_lam_re, s5_lam_im, s5_log_step, s5_b_re, s5_b_im, s5_c_re, s5_c_im, s5_d, s5_w_glu, sc_conv_w, cf_conv_w, cf_ln_g, cf_ln_b, w_branch, w_out, norm2_g, peer_w_query, peer_sub_keys, peer_u, peer_v, final_norm_g):
    bsz, s, d = x.shape
    t = bsz * s
    depth = w_in.shape[0]
    xt = x.reshape(t, d)
    for l in range(depth):
        proj = _inproj(xt, norm1_g[l].reshape(1, d), _relayout_w_in(w_in[l]))

        ya = _ssd(proj, bsz, s, ssd_conv_w[l], ssd_conv_b[l].reshape(1, -1), _pad_lanes(ssd_dt_bias[l]),
                  _pad_lanes(ssd_a_log[l]), _pad_lanes(ssd_d[l]), ssd_norm_g[l].reshape(1, -1))

        lamb_re, lamb_im, bb_re, bb_im = _s5_discretise(s5_lam_re[l], s5_lam_im[l], s5_log_step[l],
                                                        s5_b_re[l], s5_b_im[l])
        bmat_re = _block_diag(bb_re.transpose(0, 2, 1)).astype(BF16)
        bmat_im = _block_diag(bb_im.transpose(0, 2, 1)).astype(BF16)
        cmat_re = _block_diag(s5_c_re[l].transpose(0, 2, 1)).astype(BF16)
        cmat_imn = _block_diag(-s5_c_im[l].transpose(0, 2, 1)).astype(BF16)
        u_sb = proj[:, P_S5:P_S5 + S5_WIDTH].reshape(bsz, s, S5_WIDTH).transpose(1, 0, 2).reshape(t, S5_WIDTH)
        yb_sb = _s5(u_sb, lamb_re, lamb_im, bmat_re, bmat_im, cmat_re, cmat_imn,
                    s5_d[l].reshape(1, -1), s5_w_glu[l].astype(BF16))
        yb = yb_sb.reshape(s, bsz, S5_WIDTH).transpose(1, 0, 2).reshape(t, S5_WIDTH)

        yc, yd = _convs(proj, bsz, s, sc_conv_w[l], cf_conv_w[l], cf_ln_g[l].reshape(1, -1), cf_ln_b[l].reshape(1, -1))

        xt, h2, q = _merge(xt, ya, yb, yc, yd, proj, w_branch[l].astype(BF16), w_out[l].astype(BF16),
                           norm2_g[l].reshape(1, d), peer_w_query[l].astype(BF16))

        ii, jj, gg = _route(q, peer_sub_keys[l, 0].astype(BF16), peer_sub_keys[l, 1].astype(BF16))
        xt = _peer(h2, ii, jj, gg, xt, peer_u[l].astype(BF16), peer_v[l].astype(BF16),
                   final_norm_g.reshape(1, d), final=(l == depth - 1))
    return xt.reshape(bsz, s, d)
```

```python
import functools

import jax
import jax.numpy as jnp
from jax import lax
from jax.experimental import pallas as pl
from jax.experimental.pallas import tpu as pltpu

F32 = jnp.float32
BF16 = jnp.bfloat16
EPS = 1e-6

D_MODEL = 1024
SSD_HEADS = 12
SSD_HEAD_DIM = 64
SSD_INNER = SSD_HEADS * SSD_HEAD_DIM
SSD_GROUPS = 4
SSD_HPG = SSD_HEADS // SSD_GROUPS
SSD_STATE = 64
SSD_CONV = 4
SSD_CHUNK = 128
SSD_CONV_CH = SSD_INNER + 2 * SSD_GROUPS * SSD_STATE
S5_GROUP = 16
S5_WIDTH = 512
S5_GROUPS = S5_WIDTH // S5_GROUP
S5_STATE = 64
S5_LANES = S5_GROUPS * S5_STATE
SC_WIDTH = 512
SC_CONV = 3
CF_WIDTH = 512
CF_CONV = 31
N_BRANCH = 4
BRANCH_OFFSETS = (0, 768, 1280, 1792, 2304)
PEER_HEADS = 8
PEER_KEYS = 128
PEER_TOPK = 16
PEER_HALF = 128
PEER_SEL = PEER_HEADS * PEER_TOPK
N_EXPERTS = PEER_KEYS * PEER_KEYS

OFF_Z = 0
OFF_XBC = OFF_Z + SSD_INNER
OFF_DT = OFF_XBC + SSD_CONV_CH
OFF_S5 = OFF_DT + SSD_HEADS
OFF_SC = OFF_S5 + S5_WIDTH
OFF_CF = OFF_SC + 3 * SC_WIDTH
OFF_GATE = OFF_CF + 2 * CF_WIDTH
IN_COLS = OFF_GATE + N_BRANCH * D_MODEL

LANE = 128
P_ZX = 0
P_S5 = 2048
P_SC = P_S5 + S5_WIDTH
P_CF = P_SC + 3 * SC_WIDTH
P_GATE = P_CF + 2 * CF_WIDTH
P_DT = P_GATE + N_BRANCH * D_MODEL
P_COLS = 9728
PROJ_TN = 512

VMEM_LIMIT = 56 * 1024 * 1024


def _cparams(*sem):
    return pltpu.CompilerParams(dimension_semantics=sem, vmem_limit_bytes=VMEM_LIMIT)


def _dot(a, b):
    return jnp.dot(a, b, preferred_element_type=F32)


def _dot_nt(a, b):
    return lax.dot_general(a, b, (((1,), (1,)), ((), ())), preferred_element_type=F32)


def _rms(x, g):
    return x * lax.rsqrt(jnp.mean(x * x, axis=-1, keepdims=True) + EPS) * g


def _inproj_kernel(x_ref, g_ref, w_ref, o_ref, h_ref):
    @pl.when(pl.program_id(1) == 0)
    def _():
        h_ref[...] = _rms(x_ref[...], g_ref[...]).astype(BF16)

    o_ref[...] = _dot(h_ref[...], w_ref[...])


def _inproj(x, g, w):
    t = x.shape[0]
    tm = min(2048, t)
    return pl.pallas_call(
        _inproj_kernel,
        grid=(t // tm, P_COLS // PROJ_TN),
        in_specs=[pl.BlockSpec((tm, D_MODEL), lambda i, j: (i, 0)),
                  pl.BlockSpec((1, D_MODEL), lambda i, j: (0, 0)),
                  pl.BlockSpec((D_MODEL, PROJ_TN), lambda i, j: (0, j))],
        out_specs=pl.BlockSpec((tm, PROJ_TN), lambda i, j: (i, j)),
        out_shape=jax.ShapeDtypeStruct((t, P_COLS), F32),
        scratch_shapes=[pltpu.VMEM((tm, D_MODEL), BF16)],
        compiler_params=_cparams("arbitrary", "arbitrary"),
        name="inproj",
    )(x, g, w)


def _split3(x):
    hi = x.astype(BF16)
    r = x - hi.astype(F32)
    mid = r.astype(BF16)
    lo = (r - mid.astype(F32)).astype(BF16)
    return hi, mid, lo


def _ssd_kernel(zx_ref, dt_ref, cw_ref, cb_ref, dtb_ref, alog_ref, dsk_ref, ng_ref, o_ref,
                xbuf, state, ybuf):
    L = SSD_CHUNK
    c = pl.program_id(1)

    @pl.when(c == 0)
    def _():
        xbuf[0:8, :] = jnp.zeros((8, SSD_CONV_CH), F32)
        state[...] = jnp.zeros(state.shape, F32)

    xbuf[8:8 + L, :] = zx_ref[:, SSD_INNER:SSD_INNER + SSD_CONV_CH]
    conv = cb_ref[...] + cw_ref[0:1, :] * xbuf[5:5 + L, :]
    for k in range(1, SSD_CONV):
        conv = conv + cw_ref[k:k + 1, :] * xbuf[5 + k:5 + k + L, :]
    xbuf[0:8, :] = xbuf[L:L + 8, :]
    xc = jax.nn.silu(conv)

    gn = SSD_GROUPS * SSD_STATE
    bm = xc[:, SSD_INNER:SSD_INNER + gn]
    cm = xc[:, SSD_INNER + gn:SSD_INNER + 2 * gn]
    bm_b = bm.astype(BF16)
    cm_b = cm.astype(BF16)
    bmT_b = bm.T.astype(BF16)

    dt = jax.nn.softplus(dt_ref[...] + dtb_ref[...])
    a = -jnp.exp(alog_ref[...])
    adt = dt * a

    row = lax.broadcasted_iota(jnp.int32, (L, L), 0)
    col = lax.broadcasted_iota(jnp.int32, (L, L), 1)
    tril = row >= col
    tri_b = jnp.where(tril, 1.0, 0.0).astype(BF16)
    hi, mid, lo = _split3(adt)
    a_cum = _dot(tri_b, hi) + _dot(tri_b, mid) + _dot(tri_b, lo)
    a_cum_t = a_cum.T
    a_last = a_cum[L - 1:L, :]

    for g in range(SSD_GROUPS):
        cm_g = cm_b[:, g * SSD_STATE:(g + 1) * SSD_STATE]
        bm_g = bm_b[:, g * SSD_STATE:(g + 1) * SSD_STATE]
        bmT_g = bmT_b[g * SSD_STATE:(g + 1) * SSD_STATE, :]
        cbm = _dot_nt(cm_g, bm_g)
        for j in range(SSD_HPG):
            h = g * SSD_HPG + j
            xs_h = xc[:, h * SSD_HEAD_DIM:(h + 1) * SSD_HEAD_DIM]
            dt_h = dt[:, h:h + 1]
            ac_h = a_cum[:, h:h + 1]
            xdt_h = xs_h * dt_h
            dec = jnp.exp(jnp.where(tril, ac_h - a_cum_t[h:h + 1, :], -jnp.inf))
            y = _dot((cbm * dec).astype(BF16), xdt_h.astype(BF16))
            prev = state[h]
            y = y + _dot(cm_g, prev.astype(BF16)) * jnp.exp(ac_h)
            al_h = a_last[:, h:h + 1]
            w = (xdt_h * jnp.exp(al_h - ac_h)).astype(BF16)
            state[h] = jnp.exp(al_h) * prev + _dot(bmT_g, w)
            ybuf[:, h * SSD_HEAD_DIM:(h + 1) * SSD_HEAD_DIM] = y + dsk_ref[:, h:h + 1] * xs_h

    z = zx_ref[:, 0:SSD_INNER]
    o_ref[...] = _rms(ybuf[...] * jax.nn.silu(z), ng_ref[...])


def _ssd(proj, bsz, s, cw, cb, dtb, alog, dsk, ng):
    t = bsz * s
    nc = s // SSD_CHUNK
    L = SSD_CHUNK
    const = lambda b, c: (0, 0)
    return pl.pallas_call(
        _ssd_kernel,
        grid=(bsz, nc),
        in_specs=[pl.BlockSpec((L, 2048), lambda b, c: (b * nc + c, 0)),
                  pl.BlockSpec((L, LANE), lambda b, c: (b * nc + c, P_DT // LANE)),
                  pl.BlockSpec((SSD_CONV, SSD_CONV_CH), const),
                  pl.BlockSpec((1, SSD_CONV_CH), const),
                  pl.BlockSpec((1, LANE), const),
                  pl.BlockSpec((1, LANE), const),
                  pl.BlockSpec((1, LANE), const),
                  pl.BlockSpec((1, SSD_INNER), const)],
        out_specs=pl.BlockSpec((L, SSD_INNER), lambda b, c: (b * nc + c, 0)),
        out_shape=jax.ShapeDtypeStruct((t, SSD_INNER), F32),
        scratch_shapes=[pltpu.VMEM((L + 8, SSD_CONV_CH), F32),
                        pltpu.VMEM((SSD_HEADS, SSD_STATE, SSD_HEAD_DIM), F32),
                        pltpu.VMEM((L, SSD_INNER), F32)],
        compiler_params=_cparams("arbitrary", "arbitrary"),
        name="ssd",
    )(proj, proj, cw, cb, dtb, alog, dsk, ng)


def _s5_disc_kernel(lre_ref, lim_ref, ls_ref, bre_ref, bim_ref, ore_ref, oim_ref, obre_ref, obim_ref):
    lr = lre_ref[...]
    li = lim_ref[...]
    step = jnp.exp(ls_ref[...])
    mag = jnp.exp(lr * step)
    pr = mag * jnp.cos(li * step)
    pi = mag * jnp.sin(li * step)
    ore_ref[...] = pr
    oim_ref[...] = pi
    qr = pr - 1.0
    den = lr * lr + li * li
    cr = (qr * lr + pi * li) / den
    ci = (pi * lr - qr * li) / den
    br = bre_ref[...]
    bi = bim_ref[...]
    obre_ref[...] = cr * br - ci * bi
    obim_ref[...] = cr * bi + ci * br


def _s5_discretise(lam_re, lam_im, log_step, b_re, b_im):
    n = S5_STATE * S5_GROUP
    rep = lambda a: jnp.repeat(a, S5_GROUP, axis=-1)
    args = (rep(lam_re), rep(lam_im), jnp.broadcast_to(log_step[:, None], (S5_GROUPS, n)),
            b_re.reshape(S5_GROUPS, n), b_im.reshape(S5_GROUPS, n))
    outs = pl.pallas_call(
        _s5_disc_kernel,
        out_shape=[jax.ShapeDtypeStruct((S5_GROUPS, n), F32)] * 4,
        name="s5_disc",
    )(*args)
    lamb_re, lamb_im, bb_re, bb_im = outs
    lamb_re = lamb_re[:, ::S5_GROUP].reshape(1, S5_LANES)
    lamb_im = lamb_im[:, ::S5_GROUP].reshape(1, S5_LANES)
    return lamb_re, lamb_im, bb_re.reshape(S5_GROUPS, S5_STATE, S5_GROUP), bb_im.reshape(S5_GROUPS, S5_STATE, S5_GROUP)


def _block_diag(blocks):
    g, r, c = blocks.shape
    eye = jnp.eye(g, dtype=blocks.dtype)
    return (blocks[:, :, None, :] * eye[:, None, :, None]).reshape(g * r, g * c)


S5_COLS = 512


def _s5_kernel(u_ref, lre_ref, lim_ref, bre_ref, bim_ref, cre_ref, cimn_ref, d_ref, wglu_ref, o_ref,
               hre, him, sre, sim, tbuf):
    nb, steps, _ = u_ref.shape
    rows = nb * steps
    nslab = S5_WIDTH // LANE

    @pl.when(pl.program_id(0) == 0)
    def _():
        sre[...] = jnp.zeros(sre.shape, F32)
        sim[...] = jnp.zeros(sim.shape, F32)

    for b in range(nb):
        for cs in range(nslab):
            tbuf[pl.ds(cs * rows + b, steps, stride=nb), :] = u_ref[b, :, cs * LANE:(cs + 1) * LANE]
    u = jnp.concatenate([tbuf[cs * rows:(cs + 1) * rows, :] for cs in range(nslab)], axis=1)
    ub = u.astype(BF16)
    hre[...] = _dot(ub, bre_ref[...])
    him[...] = _dot(ub, bim_ref[...])

    for cb in range(S5_LANES // S5_COLS):
        cols = slice(cb * S5_COLS, (cb + 1) * S5_COLS)
        lr = jnp.broadcast_to(lre_ref[:, cols], (8, S5_COLS))
        li = jnp.broadcast_to(lim_ref[:, cols], (8, S5_COLS))

        def body(t, carry, cols=cols, lr=lr, li=li):
            hr, hi = carry
            r0 = pl.multiple_of(t * 8, 8)
            nr = lr * hr - li * hi + hre[pl.ds(r0, 8), cols]
            ni = lr * hi + li * hr + him[pl.ds(r0, 8), cols]
            hre[pl.ds(r0, 8), cols] = nr
            him[pl.ds(r0, 8), cols] = ni
            return nr, ni

        hr, hi = lax.fori_loop(0, steps, body, (sre[:, cols], sim[:, cols]), unroll=8)
        sre[:, cols] = hr
        sim[:, cols] = hi

    y = _dot(hre[...].astype(BF16), cre_ref[...]) + _dot(him[...].astype(BF16), cimn_ref[...]) + d_ref[...] * u
    y = jax.nn.gelu(y)
    y = y * jax.nn.sigmoid(_dot(y.astype(BF16), wglu_ref[...]))
    for cs in range(nslab):
        tbuf[cs * rows:(cs + 1) * rows, :] = y[:, cs * LANE:(cs + 1) * LANE]
    for b in range(nb):
        for cs in range(nslab):
            o_ref[b, :, cs * LANE:(cs + 1) * LANE] = tbuf[pl.ds(cs * rows + b, steps, stride=nb), :]


S5_STEPS = 64


def _s5(proj3, lamb_re, lamb_im, bmat_re, bmat_im, cmat_re, cmat_imn, d, wglu):
    bsz, s, _ = proj3.shape
    steps = min(S5_STEPS, s)
    rows = bsz * steps
    const = lambda i: (0, 0)
    return pl.pallas_call(
        _s5_kernel,
        grid=(s // steps,),
        in_specs=[pl.BlockSpec((bsz, steps, S5_WIDTH), lambda i: (0, i, P_S5 // S5_WIDTH)),
                  pl.BlockSpec((1, S5_LANES), const),
                  pl.BlockSpec((1, S5_LANES), const),
                  pl.BlockSpec((S5_WIDTH, S5_LANES), const),
                  pl.BlockSpec((S5_WIDTH, S5_LANES), const),
                  pl.BlockSpec((S5_LANES, S5_WIDTH), const),
                  pl.BlockSpec((S5_LANES, S5_WIDTH), const),
                  pl.BlockSpec((1, S5_WIDTH), const),
                  pl.BlockSpec((S5_WIDTH, S5_WIDTH), const)],
        out_specs=pl.BlockSpec((bsz, steps, S5_WIDTH), lambda i: (0, i, 0)),
        out_shape=jax.ShapeDtypeStruct((bsz, s, S5_WIDTH), F32),
        scratch_shapes=[pltpu.VMEM((rows, S5_LANES), F32), pltpu.VMEM((rows, S5_LANES), F32),
                        pltpu.VMEM((8, S5_LANES), F32), pltpu.VMEM((8, S5_LANES), F32),
                        pltpu.VMEM((S5_WIDTH // LANE * rows, LANE), F32)],
        compiler_params=_cparams("arbitrary"),
        name="s5",
    )(proj3, lamb_re, lamb_im, bmat_re, bmat_im, cmat_re, cmat_imn, d, wglu)


CONV_ROWS = 64
SC_HALO = 8
CF_HALO = 32


def _conv_kernel(scb_ref, scc_ref, sch_ref, cf_ref, scw_ref, cfw_ref, lng_ref, lnb_ref, oc_ref, od_ref, chbuf, vbuf):
    L = scb_ref.shape[0]
    c = pl.program_id(1)

    @pl.when(c == 0)
    def _():
        chbuf[0:SC_HALO, :] = jnp.zeros((SC_HALO, SC_WIDTH), F32)
        vbuf[0:CF_HALO, :] = jnp.zeros((CF_HALO, CF_WIDTH), F32)

    chbuf[SC_HALO:SC_HALO + L, :] = scc_ref[...] * sch_ref[...]
    a = cf_ref[:, 0:CF_WIDTH]
    vbuf[CF_HALO:CF_HALO + L, :] = a * jax.nn.sigmoid(cf_ref[:, CF_WIDTH:2 * CF_WIDTH])

    for r0 in range(0, L, CONV_ROWS):
        base = SC_HALO - (SC_CONV - 1) + r0
        acc = scw_ref[0:1, :] * chbuf[base:base + CONV_ROWS, :]
        for k in range(1, SC_CONV):
            acc = acc + scw_ref[k:k + 1, :] * chbuf[base + k:base + k + CONV_ROWS, :]
        oc_ref[r0:r0 + CONV_ROWS, :] = scb_ref[r0:r0 + CONV_ROWS, :] * acc

        base = CF_HALO - (CF_CONV - 1) + r0
        acc = cfw_ref[0:1, :] * vbuf[base:base + CONV_ROWS, :]
        for k in range(1, CF_CONV):
            acc = acc + cfw_ref[k:k + 1, :] * vbuf[base + k:base + k + CONV_ROWS, :]
        mu = jnp.mean(acc, axis=-1, keepdims=True)
        var = jnp.mean(jnp.square(acc - mu), axis=-1, keepdims=True)
        yn = (acc - mu) * lax.rsqrt(var + EPS) * lng_ref[...] + lnb_ref[...]
        od_ref[r0:r0 + CONV_ROWS, :] = jax.nn.silu(yn)

    chbuf[0:SC_HALO, :] = chbuf[L:L + SC_HALO, :]
    vbuf[0:CF_HALO, :] = vbuf[L:L + CF_HALO, :]


def _convs(proj, bsz, s, scw, cfw, lng, lnb):
    t = bsz * s
    L = min(256, s)
    nc = s // L
    const = lambda b, c: (0, 0)
    return pl.pallas_call(
        _conv_kernel,
        grid=(bsz, nc),
        in_specs=[pl.BlockSpec((L, SC_WIDTH), lambda b, c: (b * nc + c, P_SC // SC_WIDTH)),
                  pl.BlockSpec((L, SC_WIDTH), lambda b, c: (b * nc + c, P_SC // SC_WIDTH + 1)),
                  pl.BlockSpec((L, SC_WIDTH), lambda b, c: (b * nc + c, P_SC // SC_WIDTH + 2)),
                  pl.BlockSpec((L, 2 * CF_WIDTH), lambda b, c: (b * nc + c, P_CF // (2 * CF_WIDTH))),
                  pl.BlockSpec((SC_CONV, SC_WIDTH), const),
                  pl.BlockSpec((CF_CONV, CF_WIDTH), const),
                  pl.BlockSpec((1, CF_WIDTH), const),
                  pl.BlockSpec((1, CF_WIDTH), const)],
        out_specs=[pl.BlockSpec((L, SC_WIDTH), lambda b, c: (b * nc + c, 0)),
                   pl.BlockSpec((L, CF_WIDTH), lambda b, c: (b * nc + c, 0))],
        out_shape=[jax.ShapeDtypeStruct((t, SC_WIDTH), F32), jax.ShapeDtypeStruct((t, CF_WIDTH), F32)],
        scratch_shapes=[pltpu.VMEM((L + SC_HALO, SC_WIDTH), F32), pltpu.VMEM((L + CF_HALO, CF_WIDTH), F32)],
        compiler_params=_cparams("arbitrary", "arbitrary"),
        name="convs",
    )(proj, proj, proj, proj, scw, cfw, lng, lnb)


def _merge_kernel(x_ref, ya_ref, yb_ref, yc_ref, yd_ref, ga_ref, gb_ref, gc_ref, gd_ref, wb_ref, wout_ref, g2_ref,
                  wq_ref, xo_ref, h2_ref, q_ref):
    merged = None
    branches = ((ya_ref, ga_ref), (yb_ref, gb_ref), (yc_ref, gc_ref), (yd_ref, gd_ref))
    for i, (y_ref, gate_ref) in enumerate(branches):
        lo, hi = BRANCH_OFFSETS[i], BRANCH_OFFSETS[i + 1]
        t = _dot(y_ref[...].astype(BF16), wb_ref[lo:hi, :])
        t = jax.nn.sigmoid(gate_ref[...]) * t
        merged = t if merged is None else merged + t
    xn = x_ref[...] + _dot(merged.astype(BF16), wout_ref[...])
    xo_ref[...] = xn
    h2 = _rms(xn, g2_ref[...]).astype(BF16)
    h2_ref[...] = h2
    q_ref[...] = _dot(h2, wq_ref[...]).astype(BF16)


def _merge(x, ya, yb, yc, yd, proj, wb, wout, g2, wq):
    t = x.shape[0]
    tm = min(512, t)
    nq = wq.shape[1]
    row = lambda i: (i, 0)
    const = lambda i: (0, 0)
    return pl.pallas_call(
        _merge_kernel,
        grid=(t // tm,),
        in_specs=[pl.BlockSpec((tm, D_MODEL), row),
                  pl.BlockSpec((tm, SSD_INNER), row),
                  pl.BlockSpec((tm, S5_WIDTH), row),
                  pl.BlockSpec((tm, SC_WIDTH), row),
                  pl.BlockSpec((tm, CF_WIDTH), row),
                  pl.BlockSpec((tm, D_MODEL), lambda i: (i, P_GATE // D_MODEL)),
                  pl.BlockSpec((tm, D_MODEL), lambda i: (i, P_GATE // D_MODEL + 1)),
                  pl.BlockSpec((tm, D_MODEL), lambda i: (i, P_GATE // D_MODEL + 2)),
                  pl.BlockSpec((tm, D_MODEL), lambda i: (i, P_GATE // D_MODEL + 3)),
                  pl.BlockSpec((BRANCH_OFFSETS[-1], D_MODEL), const),
                  pl.BlockSpec((D_MODEL, D_MODEL), const),
                  pl.BlockSpec((1, D_MODEL), const),
                  pl.BlockSpec((D_MODEL, nq), const)],
        out_specs=[pl.BlockSpec((tm, D_MODEL), row), pl.BlockSpec((tm, D_MODEL), row), pl.BlockSpec((tm, nq), row)],
        out_shape=[jax.ShapeDtypeStruct((t, D_MODEL), F32), jax.ShapeDtypeStruct((t, D_MODEL), BF16),
                   jax.ShapeDtypeStruct((t, nq), BF16)],
        compiler_params=_cparams("arbitrary"),
        name="merge",
    )(x, ya, yb, yc, yd, proj, proj, proj, proj, wb, wout, g2, wq)


NEG_INF = float("-inf")
CAND_ROWS = 2 * PEER_TOPK + (PEER_TOPK - 2) * 8


def _top16(x, vals_ref, idx_ref, ids):
    big = jnp.float32(1e9)
    for r in range(PEER_TOPK):
        m = jnp.max(x, axis=0, keepdims=True)
        sel = jnp.min(jnp.where(x == m, ids, big), axis=0, keepdims=True)
        vals_ref[r:r + 1, :] = m
        idx_ref[r:r + 1, :] = sel
        x = jnp.where(ids == sel, NEG_INF, x)


def _route_kernel(q_ref, k1_ref, k2_ref, oi_ref, oj_ref, og_ref,
                  v1, i1, v2, i2, cand, best, pos, si, sj, sg):
    tn = q_ref.shape[0]
    key_ids = lax.broadcasted_iota(jnp.int32, (PEER_KEYS, tn), 0).astype(F32)
    r = lax.broadcasted_iota(jnp.int32, (CAND_ROWS, tn), 0)
    pos_ids = jnp.where(r < 2 * PEER_TOPK, r,
                        (2 + ((r - 2 * PEER_TOPK) >> 3)) * PEER_TOPK + ((r - 2 * PEER_TOPK) & 7)).astype(F32)

    def head(h, carry):
        c0 = pl.multiple_of(h * 2 * PEER_HALF, 2 * PEER_HALF)
        q1 = q_ref[:, pl.ds(c0, PEER_HALF)]
        q2 = q_ref[:, pl.ds(c0 + PEER_HALF, PEER_HALF)]
        _top16(_dot_nt(k1_ref[...], q1), v1, i1, key_ids)
        _top16(_dot_nt(k2_ref[...], q2), v2, i2, key_ids)
        v2a = v2[...]
        cand[0:PEER_TOPK, :] = v1[0:1, :] + v2a
        cand[PEER_TOPK:2 * PEER_TOPK, :] = v1[1:2, :] + v2a
        for a in range(2, PEER_TOPK):
            o = 2 * PEER_TOPK + (a - 2) * 8
            cand[o:o + 8, :] = v1[a:a + 1, :] + v2a[0:8, :]
        _top16(cand[...], best, pos, pos_ids)
        p = pos[...]
        a_k = jnp.floor(p * (1.0 / PEER_TOPK))
        b_k = p - a_k * PEER_TOPK
        isel = jnp.zeros((PEER_TOPK, tn), F32)
        jsel = jnp.zeros((PEER_TOPK, tn), F32)
        for a in range(PEER_TOPK):
            isel = jnp.where(a_k == a, i1[a:a + 1, :], isel)
            jsel = jnp.where(b_k == a, i2[a:a + 1, :], jsel)
        r0 = pl.multiple_of(h * PEER_TOPK, PEER_TOPK)
        si[pl.ds(r0, PEER_TOPK), :] = isel
        sj[pl.ds(r0, PEER_TOPK), :] = jsel
        sg[pl.ds(r0, PEER_TOPK), :] = jax.nn.softmax(best[...], axis=0)
        return carry

    lax.fori_loop(0, PEER_HEADS, head, 0)
    oi_ref[...] = si[...].T
    oj_ref[...] = sj[...].T
    og_ref[...] = sg[...].T


def _route(q, k1, k2):
    t = q.shape[0]
    tn = min(512, t)
    row = lambda i: (i, 0)
    const = lambda i: (0, 0)
    vm = lambda n: pltpu.VMEM((n, tn), F32)
    return pl.pallas_call(
        _route_kernel,
        grid=(t // tn,),
        in_specs=[pl.BlockSpec((tn, q.shape[1]), row),
                  pl.BlockSpec((PEER_KEYS, PEER_HALF), const),
                  pl.BlockSpec((PEER_KEYS, PEER_HALF), const)],
        out_specs=[pl.BlockSpec((tn, PEER_SEL), row)] * 3,
        out_shape=[jax.ShapeDtypeStruct((t, PEER_SEL), F32)] * 3,
        scratch_shapes=[vm(PEER_TOPK), vm(PEER_TOPK), vm(PEER_TOPK), vm(PEER_TOPK), vm(CAND_ROWS),
                        vm(PEER_TOPK), vm(PEER_TOPK), vm(PEER_SEL), vm(PEER_SEL), vm(PEER_SEL)],
        compiler_params=_cparams("arbitrary"),
        name="route",
    )(q, k1, k2)


PEER_TE = 1024
GS_PITCH = 136
GS_TOK = 32


def _peer_kernel(h_ref, ii_ref, jj_ref, gg_ref, x_ref, u_ref, v_ref, fg_ref, o_ref, gmat, acc, gs, *, final):
    tm = h_ref.shape[0]
    e = pl.program_id(1)

    @pl.when(e == 0)
    def _():
        acc[...] = jnp.zeros(acc.shape, F32)
        ids = lax.broadcasted_iota(jnp.int32, (PEER_KEYS, PEER_SEL), 0).astype(F32)

        def group(gi, carry):
            t0 = pl.multiple_of(gi * GS_TOK, GS_TOK)
            for tt in range(GS_TOK):
                irow = ii_ref[pl.ds(t0 + tt, 1), :]
                jrow = jj_ref[pl.ds(t0 + tt, 1), :]
                grow = gg_ref[pl.ds(t0 + tt, 1), :]
                pt = jnp.where(ids == irow, 1.0, 0.0).astype(BF16)
                qt = jnp.where(ids == jrow, grow, 0.0).astype(BF16)
                gs[tt * GS_PITCH:tt * GS_PITCH + PEER_KEYS, :] = _dot_nt(pt, qt)

            for i in range(PEER_KEYS):
                blk = gs[pl.ds(i, GS_TOK, stride=GS_PITCH), :]
                gmat[pl.ds(t0, GS_TOK), i * PEER_KEYS:(i + 1) * PEER_KEYS] = blk.astype(BF16)
            return carry

        lax.fori_loop(0, tm // GS_TOK, group, 0)

    s = _dot_nt(h_ref[...], u_ref[...])
    w = gmat[:, pl.ds(pl.multiple_of(e * PEER_TE, PEER_TE), PEER_TE)]
    a = (jax.nn.gelu(s) * w.astype(F32)).astype(BF16)
    acc[...] += _dot(a, v_ref[...])

    @pl.when(e == pl.num_programs(1) - 1)
    def _():
        xn = x_ref[...] + acc[...]
        o_ref[...] = _rms(xn, fg_ref[...]) if final else xn


def _peer(h2, ii, jj, gg, x, u, v, l, fg, final):
    t = h2.shape[0]
    tm = min(512, t)
    row = lambda i, e: (i, 0)
    return pl.pallas_call(
        functools.partial(_peer_kernel, final=final),
        grid=(t // tm, N_EXPERTS // PEER_TE),
        in_specs=[pl.BlockSpec((tm, D_MODEL), row),
                  pl.BlockSpec((tm, PEER_SEL), row),
                  pl.BlockSpec((tm, PEER_SEL), row),
                  pl.BlockSpec((tm, PEER_SEL), row),
                  pl.BlockSpec((tm, D_MODEL), row),
                  pl.BlockSpec((None, PEER_TE, D_MODEL), lambda i, e: (l, e, 0)),
                  pl.BlockSpec((None, PEER_TE, D_MODEL), lambda i, e: (l, e, 0)),
                  pl.BlockSpec((1, D_MODEL), lambda i, e: (0, 0))],
        out_specs=pl.BlockSpec((tm, D_MODEL), row),
        out_shape=jax.ShapeDtypeStruct((t, D_MODEL), F32),
        scratch_shapes=[pltpu.VMEM((tm, N_EXPERTS), BF16),
                        pltpu.VMEM((tm, D_MODEL), F32),
                        pltpu.VMEM((GS_TOK * GS_PITCH, PEER_KEYS), F32)],
        compiler_params=_cparams("arbitrary", "arbitrary"),
        name="peer",
    )(h2, ii, jj, gg, x, u, v, fg)


def _pad_lanes(v, n=LANE):
    return jnp.pad(v, (0, n - v.shape[0])).reshape(1, n)


N_ZX_TILES = OFF_DT // PROJ_TN
DT_TILE = P_DT // PROJ_TN


def _relayout_kernel(a_ref, b_ref, o_ref):
    j = pl.program_id(0)

    @pl.when(j < N_ZX_TILES)
    def _():
        o_ref[...] = a_ref[...].astype(BF16)

    @pl.when(jnp.logical_and(j >= N_ZX_TILES, j < DT_TILE))
    def _():
        w = jnp.concatenate([a_ref[...], b_ref[...]], axis=1)
        o_ref[...] = w[:, SSD_HEADS:SSD_HEADS + PROJ_TN].astype(BF16)

    @pl.when(j == DT_TILE)
    def _():
        lane = lax.broadcasted_iota(jnp.int32, a_ref.shape, 1)
        o_ref[...] = jnp.where(lane < SSD_HEADS, a_ref[...], 0.0).astype(BF16)


def _relayout_w_in(w, l):
    assert OFF_DT == P_S5 and IN_COLS - OFF_S5 == P_DT - P_S5 and OFF_DT % PROJ_TN == 0
    a_idx = lambda j: (l, 0, jnp.where(j == DT_TILE, N_ZX_TILES, j))
    b_idx = lambda j: (l, 0, jnp.clip(j + 1, N_ZX_TILES + 1, DT_TILE) * (PROJ_TN // LANE))
    return pl.pallas_call(
        _relayout_kernel,
        grid=(DT_TILE + 1,),
        in_specs=[pl.BlockSpec((None, D_MODEL, PROJ_TN), a_idx), pl.BlockSpec((None, D_MODEL, LANE), b_idx)],
        out_specs=pl.BlockSpec((D_MODEL, PROJ_TN), lambda j: (0, j)),
        out_shape=jax.ShapeDtypeStruct((D_MODEL, P_COLS), BF16),
        compiler_params=_cparams("arbitrary"),
        name="w_in_relayout",
    )(w, w)


def kernel(x, norm1_g, w_in, ssd_conv_w, ssd_conv_b, ssd_dt_bias, ssd_a_log, ssd_d, ssd_norm_g, s5_lam_re, s5_lam_im, s5_log_step, s5_b_re, s5_b_im, s5_c_re, s5_c_im, s5_d, s5_w_glu, sc_conv_w, cf_conv_w, cf_ln_g, cf_ln_b, w_branch, w_out, norm2_g, peer_w_query, peer_sub_keys, peer_u, peer_v, final_norm_g):
    bsz, s, d = x.shape
    t = bsz * s
    depth = w_in.shape[0]
    xt = x.reshape(t, d)
    u_b = peer_u.astype(BF16)
    v_b = peer_v.astype(BF16)
    for l in range(depth):
        proj = _inproj(xt, norm1_g[l].reshape(1, d), _relayout_w_in(w_in, l))

        ya = _ssd(proj, bsz, s, ssd_conv_w[l], ssd_conv_b[l].reshape(1, -1), _pad_lanes(ssd_dt_bias[l]),
                  _pad_lanes(ssd_a_log[l]), _pad_lanes(ssd_d[l]), ssd_norm_g[l].reshape(1, -1))

        lamb_re, lamb_im, bb_re, bb_im = _s5_discretise(s5_lam_re[l], s5_lam_im[l], s5_log_step[l],
                                                        s5_b_re[l], s5_b_im[l])
        bmat_re = _block_diag(bb_re.transpose(0, 2, 1)).astype(BF16)
        bmat_im = _block_diag(bb_im.transpose(0, 2, 1)).astype(BF16)
        cmat_re = _block_diag(s5_c_re[l].transpose(0, 2, 1)).astype(BF16)
        cmat_imn = _block_diag(-s5_c_im[l].transpose(0, 2, 1)).astype(BF16)
        yb = _s5(proj.reshape(bsz, s, P_COLS), lamb_re, lamb_im, bmat_re, bmat_im, cmat_re, cmat_imn,
                 s5_d[l].reshape(1, -1), s5_w_glu[l].astype(BF16)).reshape(t, S5_WIDTH)

        yc, yd = _convs(proj, bsz, s, sc_conv_w[l], cf_conv_w[l], cf_ln_g[l].reshape(1, -1), cf_ln_b[l].reshape(1, -1))

        xt, h2, q = _merge(xt, ya, yb, yc, yd, proj, w_branch[l].astype(BF16), w_out[l].astype(BF16),
                           norm2_g[l].reshape(1, d), peer_w_query[l].astype(BF16))

        ii, jj, gg = _route(q, peer_sub_keys[l, 0].astype(BF16), peer_sub_keys[l, 1].astype(BF16))
        xt = _peer(h2, ii, jj, gg, xt, u_b, v_b, l, final_norm_g.reshape(1, d), final=(l == depth - 1))
    return xt.reshape(bsz, s, d)
```

```python
import functools

import jax
import jax.numpy as jnp
from jax import lax
from jax.experimental import pallas as pl
from jax.experimental.pallas import tpu as pltpu

F32 = jnp.float32
BF16 = jnp.bfloat16
EPS = 1e-6

D_MODEL = 1024
SSD_HEADS = 12
SSD_HEAD_DIM = 64
SSD_INNER = SSD_HEADS * SSD_HEAD_DIM
SSD_GROUPS = 4
SSD_HPG = SSD_HEADS // SSD_GROUPS
SSD_STATE = 64
SSD_CONV = 4
SSD_CHUNK = 128
SSD_CONV_CH = SSD_INNER + 2 * SSD_GROUPS * SSD_STATE
S5_GROUP = 16
S5_WIDTH = 512
S5_GROUPS = S5_WIDTH // S5_GROUP
S5_STATE = 64
S5_LANES = S5_GROUPS * S5_STATE
SC_WIDTH = 512
SC_CONV = 3
CF_WIDTH = 512
CF_CONV = 31
N_BRANCH = 4
BRANCH_OFFSETS = (0, 768, 1280, 1792, 2304)
PEER_HEADS = 8
PEER_KEYS = 128
PEER_TOPK = 16
PEER_HALF = 128
PEER_SEL = PEER_HEADS * PEER_TOPK
N_EXPERTS = PEER_KEYS * PEER_KEYS

OFF_Z = 0
OFF_XBC = OFF_Z + SSD_INNER
OFF_DT = OFF_XBC + SSD_CONV_CH
OFF_S5 = OFF_DT + SSD_HEADS
OFF_SC = OFF_S5 + S5_WIDTH
OFF_CF = OFF_SC + 3 * SC_WIDTH
OFF_GATE = OFF_CF + 2 * CF_WIDTH
IN_COLS = OFF_GATE + N_BRANCH * D_MODEL

LANE = 128
P_ZX = 0
P_S5 = 2048
P_SC = P_S5 + S5_WIDTH
P_CF = P_SC + 3 * SC_WIDTH
P_GATE = P_CF + 2 * CF_WIDTH
P_DT = P_GATE + N_BRANCH * D_MODEL
P_COLS = 9728
PROJ_TN = 512
PROJ_DTYPE = BF16

VMEM_LIMIT = 56 * 1024 * 1024


def _cparams(*sem):
    return pltpu.CompilerParams(dimension_semantics=sem, vmem_limit_bytes=VMEM_LIMIT)


def _dot(a, b):
    return jnp.dot(a, b, preferred_element_type=F32)


def _dot_nt(a, b):
    return lax.dot_general(a, b, (((1,), (1,)), ((), ())), preferred_element_type=F32)


def _rms(x, g):
    return x * lax.rsqrt(jnp.mean(x * x, axis=-1, keepdims=True) + EPS) * g


def _inproj_kernel(x_ref, g_ref, w_ref, o_ref, h_ref):
    @pl.when(pl.program_id(1) == 0)
    def _():
        h_ref[...] = _rms(x_ref[...], g_ref[...]).astype(BF16)

    o_ref[...] = _dot(h_ref[...], w_ref[...]).astype(o_ref.dtype)


def _inproj(x, g, w):
    t = x.shape[0]
    tm = min(2048, t)
    return pl.pallas_call(
        _inproj_kernel,
        grid=(t // tm, P_COLS // PROJ_TN),
        in_specs=[pl.BlockSpec((tm, D_MODEL), lambda i, j: (i, 0)),
                  pl.BlockSpec((1, D_MODEL), lambda i, j: (0, 0)),
                  pl.BlockSpec((D_MODEL, PROJ_TN), lambda i, j: (0, j))],
        out_specs=pl.BlockSpec((tm, PROJ_TN), lambda i, j: (i, j)),
        out_shape=jax.ShapeDtypeStruct((t, P_COLS), PROJ_DTYPE),
        scratch_shapes=[pltpu.VMEM((tm, D_MODEL), BF16)],
        compiler_params=_cparams("arbitrary", "arbitrary"),
        name="inproj",
    )(x, g, w)


def _split3(x):
    hi = x.astype(BF16)
    r = x - hi.astype(F32)
    mid = r.astype(BF16)
    lo = (r - mid.astype(F32)).astype(BF16)
    return hi, mid, lo


def _ssd_kernel(zx_ref, dt_ref, cw_ref, cb_ref, dtb_ref, alog_ref, dsk_ref, ng_ref, o_ref,
                xbuf, state, ybuf):
    L = SSD_CHUNK
    c = pl.program_id(1)

    @pl.when(c == 0)
    def _():
        xbuf[0:8, :] = jnp.zeros((8, SSD_CONV_CH), F32)
        state[...] = jnp.zeros(state.shape, F32)

    xbuf[8:8 + L, :] = zx_ref[:, SSD_INNER:SSD_INNER + SSD_CONV_CH].astype(F32)
    conv = cb_ref[...] + cw_ref[0:1, :] * xbuf[5:5 + L, :]
    for k in range(1, SSD_CONV):
        conv = conv + cw_ref[k:k + 1, :] * xbuf[5 + k:5 + k + L, :]
    xbuf[0:8, :] = xbuf[L:L + 8, :]
    xc = jax.nn.silu(conv)

    gn = SSD_GROUPS * SSD_STATE
    bm = xc[:, SSD_INNER:SSD_INNER + gn]
    cm = xc[:, SSD_INNER + gn:SSD_INNER + 2 * gn]
    bm_b = bm.astype(BF16)
    cm_b = cm.astype(BF16)
    bmT_b = bm.T.astype(BF16)

    dt = jax.nn.softplus(dt_ref[...].astype(F32) + dtb_ref[...])
    a = -jnp.exp(alog_ref[...])
    adt = dt * a

    row = lax.broadcasted_iota(jnp.int32, (L, L), 0)
    col = lax.broadcasted_iota(jnp.int32, (L, L), 1)
    tril = row >= col
    tri_b = jnp.where(tril, 1.0, 0.0).astype(BF16)
    hi, mid, lo = _split3(adt)
    a_cum = _dot(tri_b, hi) + _dot(tri_b, mid) + _dot(tri_b, lo)
    a_cum_t = a_cum.T

    def spread(parts, width):
        n = SSD_HEADS * width
        r = lax.broadcasted_iota(jnp.int32, (LANE, n), 0)
        q = lax.broadcasted_iota(jnp.int32, (LANE, n), 1) // width
        e = jnp.where(r == q, 1.0, 0.0).astype(BF16)
        return _dot(parts[0], e) + _dot(parts[1], e) + _dot(parts[2], e)

    ac3 = _split3(a_cum)
    dt_x = spread(_split3(dt), SSD_HEAD_DIM)
    ac_x = spread(ac3, SSD_HEAD_DIM)
    ac_w = spread(ac3, LANE)
    xs = xc[:, 0:SSD_INNER]
    xdt = xs * dt_x
    al_x = ac_x[L - 1:L, :]
    eac = jnp.exp(ac_x)
    w_b = (xdt * jnp.exp(al_x - ac_x)).astype(BF16)
    sdec = jnp.exp(al_x)
    xdt_b = xdt.astype(BF16)

    cm_gs = [cm_b[:, g * SSD_STATE:(g + 1) * SSD_STATE] for g in range(SSD_GROUPS)]
    bmT_gs = [bmT_b[g * SSD_STATE:(g + 1) * SSD_STATE, :] for g in range(SSD_GROUPS)]
    cbms = [_dot_nt(cm_gs[g], bm_b[:, g * SSD_STATE:(g + 1) * SSD_STATE]) for g in range(SSD_GROUPS)]
    first_y = lax.broadcasted_iota(jnp.int32, (L, LANE), 1) < SSD_HEAD_DIM
    first_s = lax.broadcasted_iota(jnp.int32, (SSD_STATE, LANE), 1) < SSD_HEAD_DIM

    for pair in range(SSD_HEADS // 2):
        cols = slice(pair * LANE, (pair + 1) * LANE)
        prev = state[pair]
        prev_b = prev.astype(BF16)
        ys, news = [], []
        for h in (2 * pair, 2 * pair + 1):
            g = h // SSD_HPG
            dec = jnp.exp(jnp.where(tril, ac_w[:, h * LANE:(h + 1) * LANE] - a_cum_t[h:h + 1, :], -jnp.inf))
            y = _dot((cbms[g] * dec).astype(BF16), xdt_b[:, cols])
            ys.append(y + _dot(cm_gs[g], prev_b) * eac[:, cols])
            news.append(_dot(bmT_gs[g], w_b[:, cols]))
        state[pair] = sdec[:, cols] * prev + jnp.where(first_s, news[0], news[1])
        ybuf[:, cols] = jnp.where(first_y, ys[0], ys[1])

    z = zx_ref[:, 0:SSD_INNER].astype(F32)
    y = ybuf[...] + dsk_ref[...] * xs
    o_ref[...] = _rms(y * jax.nn.silu(z), ng_ref[...])


def _ssd(proj, bsz, s, cw, cb, dtb, alog, dsk, ng):
    t = bsz * s
    nc = s // SSD_CHUNK
    L = SSD_CHUNK
    const = lambda b, c: (0, 0)
    return pl.pallas_call(
        _ssd_kernel,
        grid=(bsz, nc),
        in_specs=[pl.BlockSpec((L, 2048), lambda b, c: (b * nc + c, 0)),
                  pl.BlockSpec((L, LANE), lambda b, c: (b * nc + c, P_DT // LANE)),
                  pl.BlockSpec((SSD_CONV, SSD_CONV_CH), const),
                  pl.BlockSpec((1, SSD_CONV_CH), const),
                  pl.BlockSpec((1, LANE), const),
                  pl.BlockSpec((1, LANE), const),
                  pl.BlockSpec((1, SSD_INNER), const),
                  pl.BlockSpec((1, SSD_INNER), const)],
        out_specs=pl.BlockSpec((L, SSD_INNER), lambda b, c: (b * nc + c, 0)),
        out_shape=jax.ShapeDtypeStruct((t, SSD_INNER), F32),
        scratch_shapes=[pltpu.VMEM((L + 8, SSD_CONV_CH), F32),
                        pltpu.VMEM((SSD_HEADS // 2, SSD_STATE, 2 * SSD_HEAD_DIM), F32),
                        pltpu.VMEM((L, SSD_INNER), F32)],
        compiler_params=_cparams("arbitrary", "arbitrary"),
        name="ssd",
    )(proj, proj, cw, cb, dtb, alog, dsk, ng)


def _s5_disc_kernel(lre_ref, lim_ref, ls_ref, bre_ref, bim_ref, ore_ref, oim_ref, obre_ref, obim_ref):
    lr = lre_ref[...]
    li = lim_ref[...]
    step = jnp.exp(ls_ref[...])
    mag = jnp.exp(lr * step)
    pr = mag * jnp.cos(li * step)
    pi = mag * jnp.sin(li * step)
    ore_ref[...] = pr
    oim_ref[...] = pi
    qr = pr - 1.0
    den = lr * lr + li * li
    cr = (qr * lr + pi * li) / den
    ci = (pi * lr - qr * li) / den
    br = bre_ref[...]
    bi = bim_ref[...]
    obre_ref[...] = cr * br - ci * bi
    obim_ref[...] = cr * bi + ci * br


def _s5_discretise(lam_re, lam_im, log_step, b_re, b_im):
    n = S5_STATE * S5_GROUP
    rep = lambda a: jnp.repeat(a, S5_GROUP, axis=-1)
    args = (rep(lam_re), rep(lam_im), jnp.broadcast_to(log_step[:, None], (S5_GROUPS, n)),
            b_re.reshape(S5_GROUPS, n), b_im.reshape(S5_GROUPS, n))
    outs = pl.pallas_call(
        _s5_disc_kernel,
        out_shape=[jax.ShapeDtypeStruct((S5_GROUPS, n), F32)] * 4,
        name="s5_disc",
    )(*args)
    lamb_re, lamb_im, bb_re, bb_im = outs
    lamb_re = lamb_re[:, ::S5_GROUP].reshape(1, S5_LANES)
    lamb_im = lamb_im[:, ::S5_GROUP].reshape(1, S5_LANES)
    return lamb_re, lamb_im, bb_re.reshape(S5_GROUPS, S5_STATE, S5_GROUP), bb_im.reshape(S5_GROUPS, S5_STATE, S5_GROUP)


def _block_diag(blocks):
    g, r, c = blocks.shape
    eye = jnp.eye(g, dtype=blocks.dtype)
    return (blocks[:, :, None, :] * eye[:, None, :, None]).reshape(g * r, g * c)


S5_COLS = 512
S5_BSPLIT = 2


def _s5_kernel(u_ref, lre_ref, lim_ref, bre_ref, bim_ref, cre_ref, cimn_ref, d_ref, wglu_ref, o_ref,
               hre, him, sre, sim, tbuf):
    nb, steps, _ = u_ref.shape
    rows = nb * steps
    nslab = S5_WIDTH // LANE

    @pl.when(pl.program_id(0) == 0)
    def _():
        sre[...] = jnp.zeros(sre.shape, F32)
        sim[...] = jnp.zeros(sim.shape, F32)

    for b in range(nb):
        for cs in range(nslab):
            tbuf[pl.ds(cs * rows + b, steps, stride=nb), :] = u_ref[b, :, cs * LANE:(cs + 1) * LANE].astype(F32)
    u = jnp.concatenate([tbuf[cs * rows:(cs + 1) * rows, :] for cs in range(nslab)], axis=1)
    ub = u.astype(BF16)
    kin = S5_WIDTH // S5_BSPLIT
    kst = S5_LANES // S5_BSPLIT
    for kb in range(S5_BSPLIT):
        hre[:, kb * kst:(kb + 1) * kst] = _dot(ub[:, kb * kin:(kb + 1) * kin],
                                               bre_ref[kb * kin:(kb + 1) * kin, kb * kst:(kb + 1) * kst])
        him[:, kb * kst:(kb + 1) * kst] = _dot(ub[:, kb * kin:(kb + 1) * kin],
                                               bim_ref[kb * kin:(kb + 1) * kin, kb * kst:(kb + 1) * kst])

    for cb in range(S5_LANES // S5_COLS):
        cols = slice(cb * S5_COLS, (cb + 1) * S5_COLS)
        lr = jnp.broadcast_to(lre_ref[:, cols], (8, S5_COLS))
        li = jnp.broadcast_to(lim_ref[:, cols], (8, S5_COLS))

        def body(t, carry, cols=cols, lr=lr, li=li):
            hr, hi = carry
            r0 = pl.multiple_of(t * 8, 8)
            nr = lr * hr - li * hi + hre[pl.ds(r0, 8), cols]
            ni = lr * hi + li * hr + him[pl.ds(r0, 8), cols]
            hre[pl.ds(r0, 8), cols] = nr
            him[pl.ds(r0, 8), cols] = ni
            return nr, ni

        hr, hi = lax.fori_loop(0, steps, body, (sre[:, cols], sim[:, cols]), unroll=8)
        sre[:, cols] = hr
        sim[:, cols] = hi

    kst = S5_LANES // nslab
    ys = []
    for cs in range(nslab):
        st = slice(cs * kst, (cs + 1) * kst)
        lanes = slice(cs * LANE, (cs + 1) * LANE)
        ys.append(_dot(hre[:, st].astype(BF16), cre_ref[st, lanes]) + _dot(him[:, st].astype(BF16), cimn_ref[st, lanes]))
    y = jnp.concatenate(ys, axis=1) + d_ref[...] * u
    y = jax.nn.gelu(y)
    y = y * jax.nn.sigmoid(_dot(y.astype(BF16), wglu_ref[...]))
    for cs in range(nslab):
        tbuf[cs * rows:(cs + 1) * rows, :] = y[:, cs * LANE:(cs + 1) * LANE]
    for b in range(nb):
        for cs in range(nslab):
            o_ref[b, :, cs * LANE:(cs + 1) * LANE] = tbuf[pl.ds(cs * rows + b, steps, stride=nb), :]


S5_STEPS = 64


def _s5(proj3, lamb_re, lamb_im, bmat_re, bmat_im, cmat_re, cmat_imn, d, wglu):
    bsz, s, _ = proj3.shape
    steps = min(S5_STEPS, s)
    rows = bsz * steps
    const = lambda i: (0, 0)
    return pl.pallas_call(
        _s5_kernel,
        grid=(s // steps,),
        in_specs=[pl.BlockSpec((bsz, steps, S5_WIDTH), lambda i: (0, i, P_S5 // S5_WIDTH)),
                  pl.BlockSpec((1, S5_LANES), const),
                  pl.BlockSpec((1, S5_LANES), const),
                  pl.BlockSpec((S5_WIDTH, S5_LANES), const),
                  pl.BlockSpec((S5_WIDTH, S5_LANES), const),
                  pl.BlockSpec((S5_LANES, S5_WIDTH), const),
                  pl.BlockSpec((S5_LANES, S5_WIDTH), const),
                  pl.BlockSpec((1, S5_WIDTH), const),
                  pl.BlockSpec((S5_WIDTH, S5_WIDTH), const)],
        out_specs=pl.BlockSpec((bsz, steps, S5_WIDTH), lambda i: (0, i, 0)),
        out_shape=jax.ShapeDtypeStruct((bsz, s, S5_WIDTH), F32),
        scratch_shapes=[pltpu.VMEM((rows, S5_LANES), F32), pltpu.VMEM((rows, S5_LANES), F32),
                        pltpu.VMEM((8, S5_LANES), F32), pltpu.VMEM((8, S5_LANES), F32),
                        pltpu.VMEM((S5_WIDTH // LANE * rows, LANE), F32)],
        compiler_params=_cparams("arbitrary"),
        name="s5",
    )(proj3, lamb_re, lamb_im, bmat_re, bmat_im, cmat_re, cmat_imn, d, wglu)


CONV_ROWS = 64


def _conv_kernel(scb_ref, scc_ref, sch_ref, cfa_ref, cfg_ref, scw_ref, cfw_ref, lng_ref, lnb_ref, oc_ref, od_ref,
                 chbuf, vbuf, ycbuf, ydbuf):
    nb, steps, _ = scb_ref.shape
    rows = nb * steps
    nslab = SC_WIDTH // LANE
    sc_halo = (SC_CONV - 1) * nb
    cf_halo = (CF_CONV - 1) * nb
    sc_pitch = sc_halo + rows
    cf_pitch = cf_halo + rows

    @pl.when(pl.program_id(0) == 0)
    def _():
        for cs in range(nslab):
            chbuf[cs * sc_pitch:cs * sc_pitch + sc_halo, :] = jnp.zeros((sc_halo, LANE), F32)
            vbuf[cs * cf_pitch:cs * cf_pitch + cf_halo, :] = jnp.zeros((cf_halo, LANE), F32)

    for b in range(nb):
        ch = scc_ref[b].astype(F32) * sch_ref[b].astype(F32)
        v = cfa_ref[b].astype(F32) * jax.nn.sigmoid(cfg_ref[b].astype(F32))
        for cs in range(nslab):
            lanes = slice(cs * LANE, (cs + 1) * LANE)
            chbuf[pl.ds(cs * sc_pitch + sc_halo + b, steps, stride=nb), :] = ch[:, lanes]
            vbuf[pl.ds(cs * cf_pitch + cf_halo + b, steps, stride=nb), :] = v[:, lanes]

    def block(rb, carry):
        r0 = pl.multiple_of(rb * CONV_ROWS, CONV_ROWS)
        for cs in range(nslab):
            lanes = slice(cs * LANE, (cs + 1) * LANE)
            acc = scw_ref[0:1, lanes] * chbuf[pl.ds(cs * sc_pitch + r0, CONV_ROWS), :]
            for k in range(1, SC_CONV):
                acc = acc + scw_ref[k:k + 1, lanes] * chbuf[pl.ds(cs * sc_pitch + k * nb + r0, CONV_ROWS), :]
            ycbuf[pl.ds(cs * rows + r0, CONV_ROWS), :] = acc
            acc = cfw_ref[0:1, lanes] * vbuf[pl.ds(cs * cf_pitch + r0, CONV_ROWS), :]
            for k in range(1, CF_CONV):
                acc = acc + cfw_ref[k:k + 1, lanes] * vbuf[pl.ds(cs * cf_pitch + k * nb + r0, CONV_ROWS), :]
            ydbuf[pl.ds(cs * rows + r0, CONV_ROWS), :] = acc
        return carry

    lax.fori_loop(0, rows // CONV_ROWS, block, 0)

    y = jnp.concatenate([ydbuf[cs * rows:(cs + 1) * rows, :] for cs in range(nslab)], axis=1)
    mu = jnp.mean(y, axis=-1, keepdims=True)
    var = jnp.mean(jnp.square(y - mu), axis=-1, keepdims=True)
    y = jax.nn.silu((y - mu) * lax.rsqrt(var + EPS) * lng_ref[...] + lnb_ref[...])
    for cs in range(nslab):
        ydbuf[cs * rows:(cs + 1) * rows, :] = y[:, cs * LANE:(cs + 1) * LANE]

    for b in range(nb):
        for cs in range(nslab):
            lanes = slice(cs * LANE, (cs + 1) * LANE)
            oc_ref[b, :, lanes] = scb_ref[b, :, lanes].astype(F32) * ycbuf[pl.ds(cs * rows + b, steps, stride=nb), :]
            od_ref[b, :, lanes] = ydbuf[pl.ds(cs * rows + b, steps, stride=nb), :]

    for cs in range(nslab):
        chbuf[cs * sc_pitch:cs * sc_pitch + sc_halo, :] = chbuf[cs * sc_pitch + rows:cs * sc_pitch + rows + sc_halo, :]
        vbuf[cs * cf_pitch:cs * cf_pitch + cf_halo, :] = vbuf[cs * cf_pitch + rows:cs * cf_pitch + rows + cf_halo, :]


CONV_STEPS = 64


def _convs(proj3, scw, cfw, lng, lnb):
    bsz, s, _ = proj3.shape
    steps = min(CONV_STEPS, s)
    rows = bsz * steps
    nslab = SC_WIDTH // LANE
    const = lambda i: (0, 0)
    blk = lambda col: pl.BlockSpec((bsz, steps, SC_WIDTH), lambda i: (0, i, col))
    out = pl.BlockSpec((bsz, steps, SC_WIDTH), lambda i: (0, i, 0))
    return pl.pallas_call(
        _conv_kernel,
        grid=(s // steps,),
        in_specs=[blk(P_SC // SC_WIDTH), blk(P_SC // SC_WIDTH + 1), blk(P_SC // SC_WIDTH + 2),
                  blk(P_CF // CF_WIDTH), blk(P_CF // CF_WIDTH + 1),
                  pl.BlockSpec((SC_CONV, SC_WIDTH), const),
                  pl.BlockSpec((CF_CONV, CF_WIDTH), const),
                  pl.BlockSpec((1, CF_WIDTH), const),
                  pl.BlockSpec((1, CF_WIDTH), const)],
        out_specs=[out, out],
        out_shape=[jax.ShapeDtypeStruct((bsz, s, SC_WIDTH), F32), jax.ShapeDtypeStruct((bsz, s, CF_WIDTH), F32)],
        scratch_shapes=[pltpu.VMEM((nslab * ((SC_CONV - 1) * bsz + rows), LANE), F32),
                        pltpu.VMEM((nslab * ((CF_CONV - 1) * bsz + rows), LANE), F32),
                        pltpu.VMEM((nslab * rows, LANE), F32),
                        pltpu.VMEM((nslab * rows, LANE), F32)],
        compiler_params=_cparams("arbitrary"),
        name="convs",
    )(proj3, proj3, proj3, proj3, proj3, scw, cfw, lng, lnb)


def _merge_kernel(x_ref, ya_ref, yb_ref, yc_ref, yd_ref, ga_ref, gb_ref, gc_ref, gd_ref, wb_ref, wout_ref, g2_ref,
                  wq_ref, xo_ref, h2_ref, q_ref):
    merged = None
    branches = ((ya_ref, ga_ref), (yb_ref, gb_ref), (yc_ref, gc_ref), (yd_ref, gd_ref))
    for i, (y_ref, gate_ref) in enumerate(branches):
        lo, hi = BRANCH_OFFSETS[i], BRANCH_OFFSETS[i + 1]
        t = _dot(y_ref[...].astype(BF16), wb_ref[lo:hi, :])
        t = jax.nn.sigmoid(gate_ref[...].astype(F32)) * t
        merged = t if merged is None else merged + t
    xn = x_ref[...] + _dot(merged.astype(BF16), wout_ref[...])
    xo_ref[...] = xn
    h2 = _rms(xn, g2_ref[...]).astype(BF16)
    h2_ref[...] = h2
    q_ref[...] = _dot(h2, wq_ref[...]).astype(BF16)


def _merge(x, ya, yb, yc, yd, proj, wb, wout, g2, wq):
    t = x.shape[0]
    tm = min(512, t)
    nq = wq.shape[1]
    row = lambda i: (i, 0)
    const = lambda i: (0, 0)
    return pl.pallas_call(
        _merge_kernel,
        grid=(t // tm,),
        in_specs=[pl.BlockSpec((tm, D_MODEL), row),
                  pl.BlockSpec((tm, SSD_INNER), row),
                  pl.BlockSpec((tm, S5_WIDTH), row),
                  pl.BlockSpec((tm, SC_WIDTH), row),
                  pl.BlockSpec((tm, CF_WIDTH), row),
                  pl.BlockSpec((tm, D_MODEL), lambda i: (i, P_GATE // D_MODEL)),
                  pl.BlockSpec((tm, D_MODEL), lambda i: (i, P_GATE // D_MODEL + 1)),
                  pl.BlockSpec((tm, D_MODEL), lambda i: (i, P_GATE // D_MODEL + 2)),
                  pl.BlockSpec((tm, D_MODEL), lambda i: (i, P_GATE // D_MODEL + 3)),
                  pl.BlockSpec((BRANCH_OFFSETS[-1], D_MODEL), const),
                  pl.BlockSpec((D_MODEL, D_MODEL), const),
                  pl.BlockSpec((1, D_MODEL), const),
                  pl.BlockSpec((D_MODEL, nq), const)],
        out_specs=[pl.BlockSpec((tm, D_MODEL), row), pl.BlockSpec((tm, D_MODEL), row), pl.BlockSpec((tm, nq), row)],
        out_shape=[jax.ShapeDtypeStruct((t, D_MODEL), F32), jax.ShapeDtypeStruct((t, D_MODEL), BF16),
                   jax.ShapeDtypeStruct((t, nq), BF16)],
        compiler_params=_cparams("arbitrary"),
        name="merge",
    )(x, ya, yb, yc, yd, proj, proj, proj, proj, wb, wout, g2, wq)


NEG_INF = float("-inf")
CAND_ROWS = 2 * PEER_TOPK + (PEER_TOPK - 2) * 8


def _top16(x, vals_ref, idx_ref, ids):
    big = jnp.float32(1e9)
    for r in range(PEER_TOPK):
        m = jnp.max(x, axis=0, keepdims=True)
        sel = jnp.min(jnp.where(x == m, ids, big), axis=0, keepdims=True)
        vals_ref[r:r + 1, :] = m
        idx_ref[r:r + 1, :] = sel
        x = jnp.where(ids == sel, NEG_INF, x)


def _route_kernel(q_ref, k1_ref, k2_ref, oi_ref, oj_ref, og_ref,
                  v1, i1, v2, i2, cand, best, pos, si, sj, sg):
    tn = q_ref.shape[0]
    key_ids = lax.broadcasted_iota(jnp.int32, (PEER_KEYS, tn), 0).astype(F32)
    r = lax.broadcasted_iota(jnp.int32, (CAND_ROWS, tn), 0)
    pos_ids = jnp.where(r < 2 * PEER_TOPK, r,
                        (2 + ((r - 2 * PEER_TOPK) >> 3)) * PEER_TOPK + ((r - 2 * PEER_TOPK) & 7)).astype(F32)

    def head(h, carry):
        c0 = pl.multiple_of(h * 2 * PEER_HALF, 2 * PEER_HALF)
        q1 = q_ref[:, pl.ds(c0, PEER_HALF)]
        q2 = q_ref[:, pl.ds(c0 + PEER_HALF, PEER_HALF)]
        _top16(_dot_nt(k1_ref[...], q1), v1, i1, key_ids)
        _top16(_dot_nt(k2_ref[...], q2), v2, i2, key_ids)
        v2a = v2[...]
        cand[0:PEER_TOPK, :] = v1[0:1, :] + v2a
        cand[PEER_TOPK:2 * PEER_TOPK, :] = v1[1:2, :] + v2a
        for a in range(2, PEER_TOPK):
            o = 2 * PEER_TOPK + (a - 2) * 8
            cand[o:o + 8, :] = v1[a:a + 1, :] + v2a[0:8, :]
        _top16(cand[...], best, pos, pos_ids)
        p = pos[...]
        a_k = jnp.floor(p * (1.0 / PEER_TOPK))
        b_k = p - a_k * PEER_TOPK
        isel = jnp.zeros((PEER_TOPK, tn), F32)
        jsel = jnp.zeros((PEER_TOPK, tn), F32)
        for a in range(PEER_TOPK):
            isel = jnp.where(a_k == a, i1[a:a + 1, :], isel)
            jsel = jnp.where(b_k == a, i2[a:a + 1, :], jsel)
        r0 = pl.multiple_of(h * PEER_TOPK, PEER_TOPK)
        si[pl.ds(r0, PEER_TOPK), :] = isel
        sj[pl.ds(r0, PEER_TOPK), :] = jsel
        sg[pl.ds(r0, PEER_TOPK), :] = jax.nn.softmax(best[...], axis=0)
        return carry

    lax.fori_loop(0, PEER_HEADS, head, 0)
    oi_ref[...] = si[...].T
    oj_ref[...] = sj[...].T
    og_ref[...] = sg[...].T


def _route(q, k1, k2):
    t = q.shape[0]
    tn = min(512, t)
    row = lambda i: (i, 0)
    const = lambda i: (0, 0)
    vm = lambda n: pltpu.VMEM((n, tn), F32)
    return pl.pallas_call(
        _route_kernel,
        grid=(t // tn,),
        in_specs=[pl.BlockSpec((tn, q.shape[1]), row),
                  pl.BlockSpec((PEER_KEYS, PEER_HALF), const),
                  pl.BlockSpec((PEER_KEYS, PEER_HALF), const)],
        out_specs=[pl.BlockSpec((tn, PEER_SEL), row)] * 3,
        out_shape=[jax.ShapeDtypeStruct((t, PEER_SEL), F32)] * 3,
        scratch_shapes=[vm(PEER_TOPK), vm(PEER_TOPK), vm(PEER_TOPK), vm(PEER_TOPK), vm(CAND_ROWS),
                        vm(PEER_TOPK), vm(PEER_TOPK), vm(PEER_SEL), vm(PEER_SEL), vm(PEER_SEL)],
        compiler_params=_cparams("arbitrary"),
        name="route",
    )(q, k1, k2)


PEER_TE = 1024
GS_PITCH = 136
GS_TOK = 32


def _peer_kernel(h_ref, ii_ref, jj_ref, gg_ref, x_ref, u_ref, v_ref, fg_ref, o_ref, gmat, acc, gs, *, final):
    tm = h_ref.shape[0]
    e = pl.program_id(1)

    @pl.when(e == 0)
    def _():
        acc[...] = jnp.zeros(acc.shape, F32)
        ids = lax.broadcasted_iota(jnp.int32, (PEER_KEYS, PEER_SEL), 0).astype(F32)

        def group(gi, carry):
            t0 = pl.multiple_of(gi * GS_TOK, GS_TOK)
            for tt in range(GS_TOK):
                irow = ii_ref[pl.ds(t0 + tt, 1), :]
                jrow = jj_ref[pl.ds(t0 + tt, 1), :]
                grow = gg_ref[pl.ds(t0 + tt, 1), :]
                pt = jnp.where(ids == irow, 1.0, 0.0).astype(BF16)
                qt = jnp.where(ids == jrow, grow, 0.0).astype(BF16)
                gs[tt * GS_PITCH:tt * GS_PITCH + PEER_KEYS, :] = _dot_nt(pt, qt)

            for i in range(PEER_KEYS):
                blk = gs[pl.ds(i, GS_TOK, stride=GS_PITCH), :]
                gmat[pl.ds(t0, GS_TOK), i * PEER_KEYS:(i + 1) * PEER_KEYS] = blk.astype(BF16)
            return carry

        lax.fori_loop(0, tm // GS_TOK, group, 0)

    s = _dot_nt(h_ref[...], u_ref[...])
    w = gmat[:, pl.ds(pl.multiple_of(e * PEER_TE, PEER_TE), PEER_TE)]
    a = (jax.nn.gelu(s) * w.astype(F32)).astype(BF16)
    acc[...] += _dot(a, v_ref[...])

    @pl.when(e == pl.num_programs(1) - 1)
    def _():
        xn = x_ref[...] + acc[...]
        o_ref[...] = _rms(xn, fg_ref[...]) if final else xn


def _peer(h2, ii, jj, gg, x, u, v, l, fg, final):
    t = h2.shape[0]
    tm = min(512, t)
    row = lambda i, e: (i, 0)
    return pl.pallas_call(
        functools.partial(_peer_kernel, final=final),
        grid=(t // tm, N_EXPERTS // PEER_TE),
        in_specs=[pl.BlockSpec((tm, D_MODEL), row),
                  pl.BlockSpec((tm, PEER_SEL), row),
                  pl.BlockSpec((tm, PEER_SEL), row),
                  pl.BlockSpec((tm, PEER_SEL), row),
                  pl.BlockSpec((tm, D_MODEL), row),
                  pl.BlockSpec((None, PEER_TE, D_MODEL), lambda i, e: (l, e, 0)),
                  pl.BlockSpec((None, PEER_TE, D_MODEL), lambda i, e: (l, e, 0)),
                  pl.BlockSpec((1, D_MODEL), lambda i, e: (0, 0))],
        out_specs=pl.BlockSpec((tm, D_MODEL), row),
        out_shape=jax.ShapeDtypeStruct((t, D_MODEL), F32),
        scratch_shapes=[pltpu.VMEM((tm, N_EXPERTS), BF16),
                        pltpu.VMEM((tm, D_MODEL), F32),
                        pltpu.VMEM((GS_TOK * GS_PITCH, PEER_KEYS), F32)],
        compiler_params=_cparams("arbitrary", "arbitrary"),
        name="peer",
    )(h2, ii, jj, gg, x, u, v, fg)


def _pad_lanes(v, n=LANE):
    return jnp.pad(v, (0, n - v.shape[0])).reshape(1, n)


N_ZX_TILES = OFF_DT // PROJ_TN
DT_TILE = P_DT // PROJ_TN


def _relayout_kernel(a_ref, b_ref, o_ref):
    j = pl.program_id(0)

    @pl.when(j < N_ZX_TILES)
    def _():
        o_ref[...] = a_ref[...].astype(BF16)

    @pl.when(jnp.logical_and(j >= N_ZX_TILES, j < DT_TILE))
    def _():
        w = jnp.concatenate([a_ref[...], b_ref[...]], axis=1)
        o_ref[...] = w[:, SSD_HEADS:SSD_HEADS + PROJ_TN].astype(BF16)

    @pl.when(j == DT_TILE)
    def _():
        lane = lax.broadcasted_iota(jnp.int32, a_ref.shape, 1)
        o_ref[...] = jnp.where(lane < SSD_HEADS, a_ref[...], 0.0).astype(BF16)


def _relayout_w_in(w, l):
    assert OFF_DT == P_S5 and IN_COLS - OFF_S5 == P_DT - P_S5 and OFF_DT % PROJ_TN == 0
    a_idx = lambda j: (l, 0, jnp.where(j == DT_TILE, N_ZX_TILES, j))
    b_idx = lambda j: (l, 0, jnp.clip(j + 1, N_ZX_TILES + 1, DT_TILE) * (PROJ_TN // LANE))
    return pl.pallas_call(
        _relayout_kernel,
        grid=(DT_TILE + 1,),
        in_specs=[pl.BlockSpec((None, D_MODEL, PROJ_TN), a_idx), pl.BlockSpec((None, D_MODEL, LANE), b_idx)],
        out_specs=pl.BlockSpec((D_MODEL, PROJ_TN), lambda j: (0, j)),
        out_shape=jax.ShapeDtypeStruct((D_MODEL, P_COLS), BF16),
        compiler_params=_cparams("arbitrary"),
        name="w_in_relayout",
    )(w, w)


def kernel(x, norm1_g, w_in, ssd_conv_w, ssd_conv_b, ssd_dt_bias, ssd_a_log, ssd_d, ssd_norm_g, s5_lam_re, s5_lam_im, s5_log_step, s5_b_re, s5_b_im, s5_c_re, s5_c_im, s5_d, s5_w_glu, sc_conv_w, cf_conv_w, cf_ln_g, cf_ln_b, w_branch, w_out, norm2_g, peer_w_query, peer_sub_keys, peer_u, peer_v, final_norm_g):
    bsz, s, d = x.shape
    t = bsz * s
    depth = w_in.shape[0]
    xt = x.reshape(t, d)
    u_b = peer_u.astype(BF16)
    v_b = peer_v.astype(BF16)
    for l in range(depth):
        proj = _inproj(xt, norm1_g[l].reshape(1, d), _relayout_w_in(w_in, l))

        ya = _ssd(proj, bsz, s, ssd_conv_w[l], ssd_conv_b[l].reshape(1, -1), _pad_lanes(ssd_dt_bias[l]),
                  _pad_lanes(ssd_a_log[l]), jnp.repeat(ssd_d[l], SSD_HEAD_DIM).reshape(1, -1),
                  ssd_norm_g[l].reshape(1, -1))

        lamb_re, lamb_im, bb_re, bb_im = _s5_discretise(s5_lam_re[l], s5_lam_im[l], s5_log_step[l],
                                                        s5_b_re[l], s5_b_im[l])
        bmat_re = _block_diag(bb_re.transpose(0, 2, 1)).astype(BF16)
        bmat_im = _block_diag(bb_im.transpose(0, 2, 1)).astype(BF16)
        cmat_re = _block_diag(s5_c_re[l].transpose(0, 2, 1)).astype(BF16)
        cmat_imn = _block_diag(-s5_c_im[l].transpose(0, 2, 1)).astype(BF16)
        proj3 = proj.reshape(bsz, s, P_COLS)
        yb = _s5(proj3, lamb_re, lamb_im, bmat_re, bmat_im, cmat_re, cmat_imn,
                 s5_d[l].reshape(1, -1), s5_w_glu[l].astype(BF16)).reshape(t, S5_WIDTH)

        yc, yd = _convs(proj3, sc_conv_w[l], cf_conv_w[l], cf_ln_g[l].reshape(1, -1), cf_ln_b[l].reshape(1, -1))
        yc = yc.reshape(t, SC_WIDTH)
        yd = yd.reshape(t, CF_WIDTH)

        xt, h2, q = _merge(xt, ya, yb, yc, yd, proj, w_branch[l].astype(BF16), w_out[l].astype(BF16),
                           norm2_g[l].reshape(1, d), peer_w_query[l].astype(BF16))

        ii, jj, gg = _route(q, peer_sub_keys[l, 0].astype(BF16), peer_sub_keys[l, 1].astype(BF16))
        xt = _peer(h2, ii, jj, gg, xt, u_b, v_b, l, final_norm_g.reshape(1, d), final=(l == depth - 1))
    return xt.reshape(bsz, s, d)
```

```python
import functools

import jax
import jax.numpy as jnp
from jax import lax
from jax.experimental import pallas as pl
from jax.experimental.pallas import tpu as pltpu

F32 = jnp.float32
BF16 = jnp.bfloat16
EPS = 1e-6

D_MODEL = 1024
SSD_HEADS = 12
SSD_HEAD_DIM = 64
SSD_INNER = SSD_HEADS * SSD_HEAD_DIM
SSD_GROUPS = 4
SSD_HPG = SSD_HEADS // SSD_GROUPS
SSD_STATE = 64
SSD_CONV = 4
SSD_CHUNK = 128
SSD_CONV_CH = SSD_INNER + 2 * SSD_GROUPS * SSD_STATE
S5_GROUP = 16
S5_WIDTH = 512
S5_GROUPS = S5_WIDTH // S5_GROUP
S5_STATE = 64
S5_LANES = S5_GROUPS * S5_STATE
SC_WIDTH = 512
SC_CONV = 3
CF_WIDTH = 512
CF_CONV = 31
N_BRANCH = 4
BRANCH_OFFSETS = (0, 768, 1280, 1792, 2304)
PEER_HEADS = 8
PEER_KEYS = 128
PEER_TOPK = 16
PEER_HALF = 128
PEER_SEL = PEER_HEADS * PEER_TOPK
N_EXPERTS = PEER_KEYS * PEER_KEYS

OFF_Z = 0
OFF_XBC = OFF_Z + SSD_INNER
OFF_DT = OFF_XBC + SSD_CONV_CH
OFF_S5 = OFF_DT + SSD_HEADS
OFF_SC = OFF_S5 + S5_WIDTH
OFF_CF = OFF_SC + 3 * SC_WIDTH
OFF_GATE = OFF_CF + 2 * CF_WIDTH
IN_COLS = OFF_GATE + N_BRANCH * D_MODEL

LANE = 128
P_ZX = 0
P_S5 = 2048
P_SC = P_S5 + S5_WIDTH
P_CF = P_SC + 3 * SC_WIDTH
P_GATE = P_CF + 2 * CF_WIDTH
P_DT = P_GATE + N_BRANCH * D_MODEL
P_COLS = 9728
PROJ_TN = 512
PROJ_DTYPE = BF16

VMEM_LIMIT = 56 * 1024 * 1024


def _cparams(*sem):
    return pltpu.CompilerParams(dimension_semantics=sem, vmem_limit_bytes=VMEM_LIMIT)


def _dot(a, b):
    return jnp.dot(a, b, preferred_element_type=F32)


def _dot_nt(a, b):
    return lax.dot_general(a, b, (((1,), (1,)), ((), ())), preferred_element_type=F32)


def _rms(x, g):
    return x * lax.rsqrt(jnp.mean(x * x, axis=-1, keepdims=True) + EPS) * g


def _inproj_kernel(x_ref, g_ref, w_ref, o_ref, h_ref):
    @pl.when(pl.program_id(1) == 0)
    def _():
        h_ref[...] = _rms(x_ref[...], g_ref[...]).astype(BF16)

    o_ref[...] = _dot(h_ref[...], w_ref[...]).astype(o_ref.dtype)


def _inproj(x, g, w):
    t = x.shape[0]
    tm = min(2048, t)
    return pl.pallas_call(
        _inproj_kernel,
        grid=(t // tm, P_COLS // PROJ_TN),
        in_specs=[pl.BlockSpec((tm, D_MODEL), lambda i, j: (i, 0)),
                  pl.BlockSpec((1, D_MODEL), lambda i, j: (0, 0)),
                  pl.BlockSpec((D_MODEL, PROJ_TN), lambda i, j: (0, j))],
        out_specs=pl.BlockSpec((tm, PROJ_TN), lambda i, j: (i, j)),
        out_shape=jax.ShapeDtypeStruct((t, P_COLS), PROJ_DTYPE),
        scratch_shapes=[pltpu.VMEM((tm, D_MODEL), BF16)],
        compiler_params=_cparams("arbitrary", "arbitrary"),
        name="inproj",
    )(x, g, w)


def _split3(x):
    hi = x.astype(BF16)
    r = x - hi.astype(F32)
    mid = r.astype(BF16)
    lo = (r - mid.astype(F32)).astype(BF16)
    return hi, mid, lo


def _ssd_kernel(zx_ref, dt_ref, cw_ref, cb_ref, dtb_ref, alog_ref, dsk_ref, ng_ref, o_ref,
                xbuf, state, ybuf):
    L = SSD_CHUNK
    c = pl.program_id(1)

    @pl.when(c == 0)
    def _():
        xbuf[0:8, :] = jnp.zeros((8, SSD_CONV_CH), F32)
        state[...] = jnp.zeros(state.shape, F32)

    xbuf[8:8 + L, :] = zx_ref[:, SSD_INNER:SSD_INNER + SSD_CONV_CH].astype(F32)
    conv = cb_ref[...] + cw_ref[0:1, :] * xbuf[5:5 + L, :]
    for k in range(1, SSD_CONV):
        conv = conv + cw_ref[k:k + 1, :] * xbuf[5 + k:5 + k + L, :]
    xbuf[0:8, :] = xbuf[L:L + 8, :]
    xc = jax.nn.silu(conv)

    gn = SSD_GROUPS * SSD_STATE
    bm = xc[:, SSD_INNER:SSD_INNER + gn]
    cm = xc[:, SSD_INNER + gn:SSD_INNER + 2 * gn]
    bm_b = bm.astype(BF16)
    cm_b = cm.astype(BF16)
    bmT_b = bm.T.astype(BF16)

    dt = jax.nn.softplus(dt_ref[...].astype(F32) + dtb_ref[...])
    a = -jnp.exp(alog_ref[...])
    adt = dt * a

    row = lax.broadcasted_iota(jnp.int32, (L, L), 0)
    col = lax.broadcasted_iota(jnp.int32, (L, L), 1)
    tril = row >= col
    tri_b = jnp.where(tril, 1.0, 0.0).astype(BF16)
    hi, mid, lo = _split3(adt)
    a_cum = _dot(tri_b, hi) + _dot(tri_b, mid) + _dot(tri_b, lo)
    a_cum_t = a_cum.T

    def spread(parts, width):
        n = SSD_HEADS * width
        r = lax.broadcasted_iota(jnp.int32, (LANE, n), 0)
        q = lax.broadcasted_iota(jnp.int32, (LANE, n), 1) // width
        e = jnp.where(r == q, 1.0, 0.0).astype(BF16)
        return _dot(parts[0], e) + _dot(parts[1], e) + _dot(parts[2], e)

    ac3 = _split3(a_cum)
    dt_x = spread(_split3(dt), SSD_HEAD_DIM)
    ac_x = spread(ac3, SSD_HEAD_DIM)
    ac_w = spread(ac3, LANE)
    xs = xc[:, 0:SSD_INNER]
    xdt = xs * dt_x
    al_x = ac_x[L - 1:L, :]
    eac = jnp.exp(ac_x)
    w_b = (xdt * jnp.exp(al_x - ac_x)).astype(BF16)
    sdec = jnp.exp(al_x)
    xdt_b = xdt.astype(BF16)

    cm_gs = [cm_b[:, g * SSD_STATE:(g + 1) * SSD_STATE] for g in range(SSD_GROUPS)]
    bmT_gs = [bmT_b[g * SSD_STATE:(g + 1) * SSD_STATE, :] for g in range(SSD_GROUPS)]
    cbms = [_dot_nt(cm_gs[g], bm_b[:, g * SSD_STATE:(g + 1) * SSD_STATE]) for g in range(SSD_GROUPS)]
    first_y = lax.broadcasted_iota(jnp.int32, (L, LANE), 1) < SSD_HEAD_DIM
    first_s = lax.broadcasted_iota(jnp.int32, (SSD_STATE, LANE), 1) < SSD_HEAD_DIM

    for pair in range(SSD_HEADS // 2):
        cols = slice(pair * LANE, (pair + 1) * LANE)
        prev = state[pair]
        prev_b = prev.astype(BF16)
        ys, news = [], []
        for h in (2 * pair, 2 * pair + 1):
            g = h // SSD_HPG
            dec = jnp.exp(jnp.where(tril, ac_w[:, h * LANE:(h + 1) * LANE] - a_cum_t[h:h + 1, :], -jnp.inf))
            y = _dot((cbms[g] * dec).astype(BF16), xdt_b[:, cols])
            ys.append(y + _dot(cm_gs[g], prev_b) * eac[:, cols])
            news.append(_dot(bmT_gs[g], w_b[:, cols]))
        state[pair] = sdec[:, cols] * prev + jnp.where(first_s, news[0], news[1])
        ybuf[:, cols] = jnp.where(first_y, ys[0], ys[1])

    z = zx_ref[:, 0:SSD_INNER].astype(F32)
    y = ybuf[...] + dsk_ref[...] * xs
    o_ref[...] = _rms(y * jax.nn.silu(z), ng_ref[...])


def _ssd(proj, bsz, s, cw, cb, dtb, alog, dsk, ng):
    t = bsz * s
    nc = s // SSD_CHUNK
    L = SSD_CHUNK
    const = lambda b, c: (0, 0)
    return pl.pallas_call(
        _ssd_kernel,
        grid=(bsz, nc),
        in_specs=[pl.BlockSpec((L, 2048), lambda b, c: (b * nc + c, 0)),
                  pl.BlockSpec((L, LANE), lambda b, c: (b * nc + c, P_DT // LANE)),
                  pl.BlockSpec((SSD_CONV, SSD_CONV_CH), const),
                  pl.BlockSpec((1, SSD_CONV_CH), const),
                  pl.BlockSpec((1, LANE), const),
                  pl.BlockSpec((1, LANE), const),
                  pl.BlockSpec((1, SSD_INNER), const),
                  pl.BlockSpec((1, SSD_INNER), const)],
        out_specs=pl.BlockSpec((L, SSD_INNER), lambda b, c: (b * nc + c, 0)),
        out_shape=jax.ShapeDtypeStruct((t, SSD_INNER), F32),
        scratch_shapes=[pltpu.VMEM((L + 8, SSD_CONV_CH), F32),
                        pltpu.VMEM((SSD_HEADS // 2, SSD_STATE, 2 * SSD_HEAD_DIM), F32),
                        pltpu.VMEM((L, SSD_INNER), F32)],
        compiler_params=_cparams("arbitrary", "arbitrary"),
        name="ssd",
    )(proj, proj, cw, cb, dtb, alog, dsk, ng)


def _s5_disc_kernel(lre_ref, lim_ref, ls_ref, bre_ref, bim_ref, ore_ref, oim_ref, obre_ref, obim_ref):
    lr = lre_ref[...]
    li = lim_ref[...]
    step = jnp.exp(ls_ref[...])
    mag = jnp.exp(lr * step)
    pr = mag * jnp.cos(li * step)
    pi = mag * jnp.sin(li * step)
    ore_ref[...] = pr
    oim_ref[...] = pi
    qr = pr - 1.0
    den = lr * lr + li * li
    cr = (qr * lr + pi * li) / den
    ci = (pi * lr - qr * li) / den
    br = bre_ref[...]
    bi = bim_ref[...]
    obre_ref[...] = cr * br - ci * bi
    obim_ref[...] = cr * bi + ci * br


def _s5_discretise(lam_re, lam_im, log_step, b_re, b_im):
    n = S5_STATE * S5_GROUP
    rep = lambda a: jnp.repeat(a, S5_GROUP, axis=-1)
    args = (rep(lam_re), rep(lam_im), jnp.broadcast_to(log_step[:, None], (S5_GROUPS, n)),
            b_re.reshape(S5_GROUPS, n), b_im.reshape(S5_GROUPS, n))
    outs = pl.pallas_call(
        _s5_disc_kernel,
        out_shape=[jax.ShapeDtypeStruct((S5_GROUPS, n), F32)] * 4,
        name="s5_disc",
    )(*args)
    lamb_re, lamb_im, bb_re, bb_im = outs
    lamb_re = lamb_re[:, ::S5_GROUP].reshape(1, S5_LANES)
    lamb_im = lamb_im[:, ::S5_GROUP].reshape(1, S5_LANES)
    return lamb_re, lamb_im, bb_re.reshape(S5_GROUPS, S5_STATE, S5_GROUP), bb_im.reshape(S5_GROUPS, S5_STATE, S5_GROUP)


def _block_diag(blocks):
    g, r, c = blocks.shape
    eye = jnp.eye(g, dtype=blocks.dtype)
    return (blocks[:, :, None, :] * eye[:, None, :, None]).reshape(g * r, g * c)


S5_COLS = 512
S5_BSPLIT = 2


def _s5_kernel(u_ref, lre_ref, lim_ref, bre_ref, bim_ref, cre_ref, cimn_ref, d_ref, wglu_ref, o_ref,
               hre, him, sre, sim, tbuf):
    nb, steps, _ = u_ref.shape
    rows = nb * steps
    nslab = S5_WIDTH // LANE

    @pl.when(pl.program_id(0) == 0)
    def _():
        sre[...] = jnp.zeros(sre.shape, F32)
        sim[...] = jnp.zeros(sim.shape, F32)

    for b in range(nb):
        for cs in range(nslab):
            tbuf[pl.ds(cs * rows + b, steps, stride=nb), :] = u_ref[b, :, cs * LANE:(cs + 1) * LANE].astype(F32)
    u = jnp.concatenate([tbuf[cs * rows:(cs + 1) * rows, :] for cs in range(nslab)], axis=1)
    ub = u.astype(BF16)
    kin = S5_WIDTH // S5_BSPLIT
    kst = S5_LANES // S5_BSPLIT
    for kb in range(S5_BSPLIT):
        hre[:, kb * kst:(kb + 1) * kst] = _dot(ub[:, kb * kin:(kb + 1) * kin],
                                               bre_ref[kb * kin:(kb + 1) * kin, kb * kst:(kb + 1) * kst])
        him[:, kb * kst:(kb + 1) * kst] = _dot(ub[:, kb * kin:(kb + 1) * kin],
                                               bim_ref[kb * kin:(kb + 1) * kin, kb * kst:(kb + 1) * kst])

    for cb in range(S5_LANES // S5_COLS):
        cols = slice(cb * S5_COLS, (cb + 1) * S5_COLS)
        lr = jnp.broadcast_to(lre_ref[:, cols], (8, S5_COLS))
        li = jnp.broadcast_to(lim_ref[:, cols], (8, S5_COLS))

        def body(t, carry, cols=cols, lr=lr, li=li):
            hr, hi = carry
            r0 = pl.multiple_of(t * 8, 8)
            nr = lr * hr - li * hi + hre[pl.ds(r0, 8), cols]
            ni = lr * hi + li * hr + him[pl.ds(r0, 8), cols]
            hre[pl.ds(r0, 8), cols] = nr
            him[pl.ds(r0, 8), cols] = ni
            return nr, ni

        hr, hi = lax.fori_loop(0, steps, body, (sre[:, cols], sim[:, cols]), unroll=8)
        sre[:, cols] = hr
        sim[:, cols] = hi

    kst = S5_LANES // nslab
    ys = []
    for cs in range(nslab):
        st = slice(cs * kst, (cs + 1) * kst)
        lanes = slice(cs * LANE, (cs + 1) * LANE)
        ys.append(_dot(hre[:, st].astype(BF16), cre_ref[st, lanes]) + _dot(him[:, st].astype(BF16), cimn_ref[st, lanes]))
    y = jnp.concatenate(ys, axis=1) + d_ref[...] * u
    y = jax.nn.gelu(y)
    y = y * jax.nn.sigmoid(_dot(y.astype(BF16), wglu_ref[...]))
    for cs in range(nslab):
        tbuf[cs * rows:(cs + 1) * rows, :] = y[:, cs * LANE:(cs + 1) * LANE]
    for b in range(nb):
        for cs in range(nslab):
            o_ref[b, :, cs * LANE:(cs + 1) * LANE] = tbuf[pl.ds(cs * rows + b, steps, stride=nb), :]


S5_STEPS = 64


def _s5(proj3, lamb_re, lamb_im, bmat_re, bmat_im, cmat_re, cmat_imn, d, wglu):
    bsz, s, _ = proj3.shape
    steps = min(S5_STEPS, s)
    rows = bsz * steps
    const = lambda i: (0, 0)
    return pl.pallas_call(
        _s5_kernel,
        grid=(s // steps,),
        in_specs=[pl.BlockSpec((bsz, steps, S5_WIDTH), lambda i: (0, i, P_S5 // S5_WIDTH)),
                  pl.BlockSpec((1, S5_LANES), const),
                  pl.BlockSpec((1, S5_LANES), const),
                  pl.BlockSpec((S5_WIDTH, S5_LANES), const),
                  pl.BlockSpec((S5_WIDTH, S5_LANES), const),
                  pl.BlockSpec((S5_LANES, S5_WIDTH), const),
                  pl.BlockSpec((S5_LANES, S5_WIDTH), const),
                  pl.BlockSpec((1, S5_WIDTH), const),
                  pl.BlockSpec((S5_WIDTH, S5_WIDTH), const)],
        out_specs=pl.BlockSpec((bsz, steps, S5_WIDTH), lambda i: (0, i, 0)),
        out_shape=jax.ShapeDtypeStruct((bsz, s, S5_WIDTH), F32),
        scratch_shapes=[pltpu.VMEM((rows, S5_LANES), F32), pltpu.VMEM((rows, S5_LANES), F32),
                        pltpu.VMEM((8, S5_LANES), F32), pltpu.VMEM((8, S5_LANES), F32),
                        pltpu.VMEM((S5_WIDTH // LANE * rows, LANE), F32)],
        compiler_params=_cparams("arbitrary"),
        name="s5",
    )(proj3, lamb_re, lamb_im, bmat_re, bmat_im, cmat_re, cmat_imn, d, wglu)


CONV_ROWS = 64


def _conv_kernel(scb_ref, scc_ref, sch_ref, cfa_ref, cfg_ref, scw_ref, cfw_ref, lng_ref, lnb_ref, oc_ref, od_ref,
                 chbuf, vbuf, ycbuf, ydbuf):
    nb, steps, _ = scb_ref.shape
    rows = nb * steps
    nslab = SC_WIDTH // LANE
    sc_halo = (SC_CONV - 1) * nb
    cf_halo = (CF_CONV - 1) * nb
    sc_pitch = sc_halo + rows
    cf_pitch = cf_halo + rows

    @pl.when(pl.program_id(0) == 0)
    def _():
        for cs in range(nslab):
            chbuf[cs * sc_pitch:cs * sc_pitch + sc_halo, :] = jnp.zeros((sc_halo, LANE), F32)
            vbuf[cs * cf_pitch:cs * cf_pitch + cf_halo, :] = jnp.zeros((cf_halo, LANE), F32)

    for b in range(nb):
        ch = scc_ref[b].astype(F32) * sch_ref[b].astype(F32)
        v = cfa_ref[b].astype(F32) * jax.nn.sigmoid(cfg_ref[b].astype(F32))
        for cs in range(nslab):
            lanes = slice(cs * LANE, (cs + 1) * LANE)
            chbuf[pl.ds(cs * sc_pitch + sc_halo + b, steps, stride=nb), :] = ch[:, lanes]
            vbuf[pl.ds(cs * cf_pitch + cf_halo + b, steps, stride=nb), :] = v[:, lanes]

    def block(rb, carry):
        r0 = pl.multiple_of(rb * CONV_ROWS, CONV_ROWS)
        for cs in range(nslab):
            lanes = slice(cs * LANE, (cs + 1) * LANE)
            acc = scw_ref[0:1, lanes] * chbuf[pl.ds(cs * sc_pitch + r0, CONV_ROWS), :]
            for k in range(1, SC_CONV):
                acc = acc + scw_ref[k:k + 1, lanes] * chbuf[pl.ds(cs * sc_pitch + k * nb + r0, CONV_ROWS), :]
            ycbuf[pl.ds(cs * rows + r0, CONV_ROWS), :] = acc
            acc = cfw_ref[0:1, lanes] * vbuf[pl.ds(cs * cf_pitch + r0, CONV_ROWS), :]
            for k in range(1, CF_CONV):
                acc = acc + cfw_ref[k:k + 1, lanes] * vbuf[pl.ds(cs * cf_pitch + k * nb + r0, CONV_ROWS), :]
            ydbuf[pl.ds(cs * rows + r0, CONV_ROWS), :] = acc
        return carry

    lax.fori_loop(0, rows // CONV_ROWS, block, 0)

    y = jnp.concatenate([ydbuf[cs * rows:(cs + 1) * rows, :] for cs in range(nslab)], axis=1)
    mu = jnp.mean(y, axis=-1, keepdims=True)
    var = jnp.mean(jnp.square(y - mu), axis=-1, keepdims=True)
    y = jax.nn.silu((y - mu) * lax.rsqrt(var + EPS) * lng_ref[...] + lnb_ref[...])
    for cs in range(nslab):
        ydbuf[cs * rows:(cs + 1) * rows, :] = y[:, cs * LANE:(cs + 1) * LANE]

    for b in range(nb):
        for cs in range(nslab):
            lanes = slice(cs * LANE, (cs + 1) * LANE)
            oc_ref[b, :, lanes] = scb_ref[b, :, lanes].astype(F32) * ycbuf[pl.ds(cs * rows + b, steps, stride=nb), :]
            od_ref[b, :, lanes] = ydbuf[pl.ds(cs * rows + b, steps, stride=nb), :]

    for cs in range(nslab):
        chbuf[cs * sc_pitch:cs * sc_pitch + sc_halo, :] = chbuf[cs * sc_pitch + rows:cs * sc_pitch + rows + sc_halo, :]
        vbuf[cs * cf_pitch:cs * cf_pitch + cf_halo, :] = vbuf[cs * cf_pitch + rows:cs * cf_pitch + rows + cf_halo, :]


CONV_STEPS = 64


def _convs(proj3, scw, cfw, lng, lnb):
    bsz, s, _ = proj3.shape
    steps = min(CONV_STEPS, s)
    rows = bsz * steps
    nslab = SC_WIDTH // LANE
    const = lambda i: (0, 0)
    blk = lambda col: pl.BlockSpec((bsz, steps, SC_WIDTH), lambda i: (0, i, col))
    out = pl.BlockSpec((bsz, steps, SC_WIDTH), lambda i: (0, i, 0))
    return pl.pallas_call(
        _conv_kernel,
        grid=(s // steps,),
        in_specs=[blk(P_SC // SC_WIDTH), blk(P_SC // SC_WIDTH + 1), blk(P_SC // SC_WIDTH + 2),
                  blk(P_CF // CF_WIDTH), blk(P_CF // CF_WIDTH + 1),
                  pl.BlockSpec((SC_CONV, SC_WIDTH), const),
                  pl.BlockSpec((CF_CONV, CF_WIDTH), const),
                  pl.BlockSpec((1, CF_WIDTH), const),
                  pl.BlockSpec((1, CF_WIDTH), const)],
        out_specs=[out, out],
        out_shape=[jax.ShapeDtypeStruct((bsz, s, SC_WIDTH), F32), jax.ShapeDtypeStruct((bsz, s, CF_WIDTH), F32)],
        scratch_shapes=[pltpu.VMEM((nslab * ((SC_CONV - 1) * bsz + rows), LANE), F32),
                        pltpu.VMEM((nslab * ((CF_CONV - 1) * bsz + rows), LANE), F32),
                        pltpu.VMEM((nslab * rows, LANE), F32),
                        pltpu.VMEM((nslab * rows, LANE), F32)],
        compiler_params=_cparams("arbitrary"),
        name="convs",
    )(proj3, proj3, proj3, proj3, proj3, scw, cfw, lng, lnb)


def _merge_kernel(x_ref, ya_ref, yb_ref, yc_ref, yd_ref, ga_ref, gb_ref, gc_ref, gd_ref, wb_ref, wout_ref, g2_ref,
                  wq_ref, xo_ref, h2_ref, q_ref):
    merged = None
    branches = ((ya_ref, ga_ref), (yb_ref, gb_ref), (yc_ref, gc_ref), (yd_ref, gd_ref))
    for i, (y_ref, gate_ref) in enumerate(branches):
        lo, hi = BRANCH_OFFSETS[i], BRANCH_OFFSETS[i + 1]
        t = _dot(y_ref[...].astype(BF16), wb_ref[lo:hi, :])
        t = jax.nn.sigmoid(gate_ref[...].astype(F32)) * t
        merged = t if merged is None else merged + t
    xn = x_ref[...] + _dot(merged.astype(BF16), wout_ref[...])
    xo_ref[...] = xn
    h2 = _rms(xn, g2_ref[...]).astype(BF16)
    h2_ref[...] = h2
    q_ref[...] = _dot(h2, wq_ref[...]).astype(BF16)


def _merge(x, ya, yb, yc, yd, proj, wb, wout, g2, wq):
    t = x.shape[0]
    tm = min(512, t)
    nq = wq.shape[1]
    row = lambda i: (i, 0)
    const = lambda i: (0, 0)
    return pl.pallas_call(
        _merge_kernel,
        grid=(t // tm,),
        in_specs=[pl.BlockSpec((tm, D_MODEL), row),
                  pl.BlockSpec((tm, SSD_INNER), row),
                  pl.BlockSpec((tm, S5_WIDTH), row),
                  pl.BlockSpec((tm, SC_WIDTH), row),
                  pl.BlockSpec((tm, CF_WIDTH), row),
                  pl.BlockSpec((tm, D_MODEL), lambda i: (i, P_GATE // D_MODEL)),
                  pl.BlockSpec((tm, D_MODEL), lambda i: (i, P_GATE // D_MODEL + 1)),
                  pl.BlockSpec((tm, D_MODEL), lambda i: (i, P_GATE // D_MODEL + 2)),
                  pl.BlockSpec((tm, D_MODEL), lambda i: (i, P_GATE // D_MODEL + 3)),
                  pl.BlockSpec((BRANCH_OFFSETS[-1], D_MODEL), const),
                  pl.BlockSpec((D_MODEL, D_MODEL), const),
                  pl.BlockSpec((1, D_MODEL), const),
                  pl.BlockSpec((D_MODEL, nq), const)],
        out_specs=[pl.BlockSpec((tm, D_MODEL), row), pl.BlockSpec((tm, D_MODEL), row), pl.BlockSpec((tm, nq), row)],
        out_shape=[jax.ShapeDtypeStruct((t, D_MODEL), F32), jax.ShapeDtypeStruct((t, D_MODEL), BF16),
                   jax.ShapeDtypeStruct((t, nq), BF16)],
        compiler_params=_cparams("arbitrary"),
        name="merge",
    )(x, ya, yb, yc, yd, proj, proj, proj, proj, wb, wout, g2, wq)


NEG_INF = float("-inf")
CAND_B0, CAND_B1, CAND_A0, CAND_A1, CAND_ROWS = 0, 16, 24, 40, 72
CAND_LAST_A = 4


def _top16(x, vals_ref, idx_ref, ids):
    big = jnp.float32(1e9)
    for r in range(PEER_TOPK):
        m = jnp.max(x, axis=0, keepdims=True)
        sel = jnp.min(jnp.where(x == m, ids, big), axis=0, keepdims=True)
        vals_ref[r:r + 1, :] = m
        idx_ref[r:r + 1, :] = sel
        x = jnp.where(ids == sel, NEG_INF, x)


def _route_kernel(q_ref, k1_ref, k2_ref, oi_ref, oj_ref, og_ref,
                  v1, i1, v2, i2, cand, best, pos, si, sj, sg):
    tn = q_ref.shape[0]
    key_ids = lax.broadcasted_iota(jnp.int32, (PEER_KEYS, tn), 0).astype(F32)
    r = lax.broadcasted_iota(jnp.int32, (CAND_ROWS, tn), 0)
    ra = r - CAND_A1
    b_in_a = jnp.where(r < CAND_A1, r - CAND_A0, ra & 7)
    pos_ids = jnp.where(r < CAND_B1, r * PEER_TOPK,
                        jnp.where(r < CAND_A0, (r - CAND_B1) * PEER_TOPK + 1,
                                  jnp.where(r < CAND_A1, b_in_a, (1 + (ra >> 3)) * PEER_TOPK + b_in_a))).astype(F32)
    cand_dup = jnp.logical_and(r >= CAND_A0, b_in_a < 2)

    def head(h, carry):
        c0 = pl.multiple_of(h * 2 * PEER_HALF, 2 * PEER_HALF)
        q1 = q_ref[:, pl.ds(c0, PEER_HALF)]
        q2 = q_ref[:, pl.ds(c0 + PEER_HALF, PEER_HALF)]
        _top16(_dot_nt(k1_ref[...], q1), v1, i1, key_ids)
        _top16(_dot_nt(k2_ref[...], q2), v2, i2, key_ids)
        v1a = v1[...]
        v2a = v2[...]
        cand[CAND_B0:CAND_B1, :] = v1a + v2a[0:1, :]
        cand[CAND_B1:CAND_A0, :] = v1a[0:8, :] + v2a[1:2, :]
        cand[CAND_A0:CAND_A1, :] = v1a[0:1, :] + v2a
        for a in range(1, CAND_LAST_A + 1):
            o = CAND_A1 + (a - 1) * 8
            cand[o:o + 8, :] = v1a[a:a + 1, :] + v2a[0:8, :]
        _top16(jnp.where(cand_dup, NEG_INF, cand[...]), best, pos, pos_ids)
        p = pos[...]
        a_k = jnp.floor(p * (1.0 / PEER_TOPK))
        b_k = p - a_k * PEER_TOPK
        isel = jnp.zeros((PEER_TOPK, tn), F32)
        jsel = jnp.zeros((PEER_TOPK, tn), F32)
        for a in range(PEER_TOPK):
            isel = jnp.where(a_k == a, i1[a:a + 1, :], isel)
            jsel = jnp.where(b_k == a, i2[a:a + 1, :], jsel)
        r0 = pl.multiple_of(h * PEER_TOPK, PEER_TOPK)
        si[pl.ds(r0, PEER_TOPK), :] = isel
        sj[pl.ds(r0, PEER_TOPK), :] = jsel
        sg[pl.ds(r0, PEER_TOPK), :] = jax.nn.softmax(best[...], axis=0)
        return carry

    lax.fori_loop(0, PEER_HEADS, head, 0)
    oi_ref[...] = si[...].T
    oj_ref[...] = sj[...].T
    og_ref[...] = sg[...].T


def _route(q, k1, k2):
    t = q.shape[0]
    tn = min(1024, t)
    row = lambda i: (i, 0)
    const = lambda i: (0, 0)
    vm = lambda n: pltpu.VMEM((n, tn), F32)
    return pl.pallas_call(
        _route_kernel,
        grid=(t // tn,),
        in_specs=[pl.BlockSpec((tn, q.shape[1]), row),
                  pl.BlockSpec((PEER_KEYS, PEER_HALF), const),
                  pl.BlockSpec((PEER_KEYS, PEER_HALF), const)],
        out_specs=[pl.BlockSpec((tn, PEER_SEL), row)] * 3,
        out_shape=[jax.ShapeDtypeStruct((t, PEER_SEL), F32)] * 3,
        scratch_shapes=[vm(PEER_TOPK), vm(PEER_TOPK), vm(PEER_TOPK), vm(PEER_TOPK), vm(CAND_ROWS),
                        vm(PEER_TOPK), vm(PEER_TOPK), vm(PEER_SEL), vm(PEER_SEL), vm(PEER_SEL)],
        compiler_params=_cparams("arbitrary"),
        name="route",
    )(q, k1, k2)


PEER_TE = 1024
GS_PITCH = 136
GS_TOK = 32


def _peer_kernel(h_ref, ii_ref, jj_ref, gg_ref, x_ref, u_ref, v_ref, fg_ref, o_ref, gmat, acc, gs, *, final):
    tm = h_ref.shape[0]
    e = pl.program_id(1)

    @pl.when(e == 0)
    def _():
        acc[...] = jnp.zeros(acc.shape, F32)
        ids = lax.broadcasted_iota(jnp.int32, (PEER_KEYS, PEER_SEL), 0).astype(F32)

        def group(gi, carry):
            t0 = pl.multiple_of(gi * GS_TOK, GS_TOK)
            for tt in range(GS_TOK):
                irow = ii_ref[pl.ds(t0 + tt, 1), :]
                jrow = jj_ref[pl.ds(t0 + tt, 1), :]
                grow = gg_ref[pl.ds(t0 + tt, 1), :]
                pt = jnp.where(ids == irow, 1.0, 0.0).astype(BF16)
                qt = jnp.where(ids == jrow, grow, 0.0).astype(BF16)
                gs[tt * GS_PITCH:tt * GS_PITCH + PEER_KEYS, :] = _dot_nt(pt, qt)

            for i in range(PEER_KEYS):
                blk = gs[pl.ds(i, GS_TOK, stride=GS_PITCH), :]
                gmat[pl.ds(t0, GS_TOK), i * PEER_KEYS:(i + 1) * PEER_KEYS] = blk.astype(BF16)
            return carry

        lax.fori_loop(0, tm // GS_TOK, group, 0)

    s = _dot_nt(h_ref[...], u_ref[...])
    w = gmat[:, pl.ds(pl.multiple_of(e * PEER_TE, PEER_TE), PEER_TE)]
    a = (jax.nn.gelu(s) * w.astype(F32)).astype(BF16)
    acc[...] += _dot(a, v_ref[...])

    @pl.when(e == pl.num_programs(1) - 1)
    def _():
        xn = x_ref[...] + acc[...]
        o_ref[...] = _rms(xn, fg_ref[...]) if final else xn


def _peer(h2, ii, jj, gg, x, u, v, l, fg, final):
    t = h2.shape[0]
    tm = min(512, t)
    row = lambda i, e: (i, 0)
    return pl.pallas_call(
        functools.partial(_peer_kernel, final=final),
        grid=(t // tm, N_EXPERTS // PEER_TE),
        in_specs=[pl.BlockSpec((tm, D_MODEL), row),
                  pl.BlockSpec((tm, PEER_SEL), row),
                  pl.BlockSpec((tm, PEER_SEL), row),
                  pl.BlockSpec((tm, PEER_SEL), row),
                  pl.BlockSpec((tm, D_MODEL), row),
                  pl.BlockSpec((None, PEER_TE, D_MODEL), lambda i, e: (l, e, 0)),
                  pl.BlockSpec((None, PEER_TE, D_MODEL), lambda i, e: (l, e, 0)),
                  pl.BlockSpec((1, D_MODEL), lambda i, e: (0, 0))],
        out_specs=pl.BlockSpec((tm, D_MODEL), row),
        out_shape=jax.ShapeDtypeStruct((t, D_MODEL), F32),
        scratch_shapes=[pltpu.VMEM((tm, N_EXPERTS), BF16),
                        pltpu.VMEM((tm, D_MODEL), F32),
                        pltpu.VMEM((GS_TOK * GS_PITCH, PEER_KEYS), F32)],
        compiler_params=_cparams("arbitrary", "arbitrary"),
        name="peer",
    )(h2, ii, jj, gg, x, u, v, fg)


def _pad_lanes(v, n=LANE):
    return jnp.pad(v, (0, n - v.shape[0])).reshape(1, n)


N_ZX_TILES = OFF_DT // PROJ_TN
DT_TILE = P_DT // PROJ_TN


def _relayout_kernel(a_ref, b_ref, o_ref):
    j = pl.program_id(0)

    @pl.when(j < N_ZX_TILES)
    def _():
        o_ref[...] = a_ref[...].astype(BF16)

    @pl.when(jnp.logical_and(j >= N_ZX_TILES, j < DT_TILE))
    def _():
        w = jnp.concatenate([a_ref[...], b_ref[...]], axis=1)
        o_ref[...] = w[:, SSD_HEADS:SSD_HEADS + PROJ_TN].astype(BF16)

    @pl.when(j == DT_TILE)
    def _():
        lane = lax.broadcasted_iota(jnp.int32, a_ref.shape, 1)
        o_ref[...] = jnp.where(lane < SSD_HEADS, a_ref[...], 0.0).astype(BF16)


def _relayout_w_in(w, l):
    assert OFF_DT == P_S5 and IN_COLS - OFF_S5 == P_DT - P_S5 and OFF_DT % PROJ_TN == 0
    a_idx = lambda j: (l, 0, jnp.where(j == DT_TILE, N_ZX_TILES, j))
    b_idx = lambda j: (l, 0, jnp.clip(j + 1, N_ZX_TILES + 1, DT_TILE) * (PROJ_TN // LANE))
    return pl.pallas_call(
        _relayout_kernel,
        grid=(DT_TILE + 1,),
        in_specs=[pl.BlockSpec((None, D_MODEL, PROJ_TN), a_idx), pl.BlockSpec((None, D_MODEL, LANE), b_idx)],
        out_specs=pl.BlockSpec((D_MODEL, PROJ_TN), lambda j: (0, j)),
        out_shape=jax.ShapeDtypeStruct((D_MODEL, P_COLS), BF16),
        compiler_params=_cparams("arbitrary"),
        name="w_in_relayout",
    )(w, w)


def kernel(x, norm1_g, w_in, ssd_conv_w, ssd_conv_b, ssd_dt_bias, ssd_a_log, ssd_d, ssd_norm_g, s5_lam_re, s5_lam_im, s5_log_step, s5_b_re, s5_b_im, s5_c_re, s5_c_im, s5_d, s5_w_glu, sc_conv_w, cf_conv_w, cf_ln_g, cf_ln_b, w_branch, w_out, norm2_g, peer_w_query, peer_sub_keys, peer_u, peer_v, final_norm_g):
    bsz, s, d = x.shape
    t = bsz * s
    depth = w_in.shape[0]
    xt = x.reshape(t, d)
    u_b = peer_u.astype(BF16)
    v_b = peer_v.astype(BF16)
    for l in range(depth):
        proj = _inproj(xt, norm1_g[l].reshape(1, d), _relayout_w_in(w_in, l))

        ya = _ssd(proj, bsz, s, ssd_conv_w[l], ssd_conv_b[l].reshape(1, -1), _pad_lanes(ssd_dt_bias[l]),
                  _pad_lanes(ssd_a_log[l]), jnp.repeat(ssd_d[l], SSD_HEAD_DIM).reshape(1, -1),
                  ssd_norm_g[l].reshape(1, -1))

        lamb_re, lamb_im, bb_re, bb_im = _s5_discretise(s5_lam_re[l], s5_lam_im[l], s5_log_step[l],
                                                        s5_b_re[l], s5_b_im[l])
        bmat_re = _block_diag(bb_re.transpose(0, 2, 1)).astype(BF16)
        bmat_im = _block_diag(bb_im.transpose(0, 2, 1)).astype(BF16)
        cmat_re = _block_diag(s5_c_re[l].transpose(0, 2, 1)).astype(BF16)
        cmat_imn = _block_diag(-s5_c_im[l].transpose(0, 2, 1)).astype(BF16)
        proj3 = proj.reshape(bsz, s, P_COLS)
        yb = _s5(proj3, lamb_re, lamb_im, bmat_re, bmat_im, cmat_re, cmat_imn,
                 s5_d[l].reshape(1, -1), s5_w_glu[l].astype(BF16)).reshape(t, S5_WIDTH)

        yc, yd = _convs(proj3, sc_conv_w[l], cf_conv_w[l], cf_ln_g[l].reshape(1, -1), cf_ln_b[l].reshape(1, -1))
        yc = yc.reshape(t, SC_WIDTH)
        yd = yd.reshape(t, CF_WIDTH)

        xt, h2, q = _merge(xt, ya, yb, yc, yd, proj, w_branch[l].astype(BF16), w_out[l].astype(BF16),
                           norm2_g[l].reshape(1, d), peer_w_query[l].astype(BF16))

        ii, jj, gg = _route(q, peer_sub_keys[l, 0].astype(BF16), peer_sub_keys[l, 1].astype(BF16))
        xt = _peer(h2, ii, jj, gg, xt, u_b, v_b, l, final_norm_g.reshape(1, d), final=(l == depth - 1))
    return xt.reshape(bsz, s, d)
```

```python
import functools

import jax
import jax.numpy as jnp
from jax import lax
from jax.experimental import pallas as pl
from jax.experimental.pallas import tpu as pltpu

F32 = jnp.float32
BF16 = jnp.bfloat16
EPS = 1e-6

D_MODEL = 1024
SSD_HEADS = 12
SSD_HEAD_DIM = 64
SSD_INNER = SSD_HEADS * SSD_HEAD_DIM
SSD_GROUPS = 4
SSD_HPG = SSD_HEADS // SSD_GROUPS
SSD_STATE = 64
SSD_CONV = 4
SSD_CHUNK = 128
SSD_CONV_CH = SSD_INNER + 2 * SSD_GROUPS * SSD_STATE
S5_GROUP = 16
S5_WIDTH = 512
S5_GROUPS = S5_WIDTH // S5_GROUP
S5_STATE = 64
S5_LANES = S5_GROUPS * S5_STATE
SC_WIDTH = 512
SC_CONV = 3
CF_WIDTH = 512
CF_CONV = 31
N_BRANCH = 4
BRANCH_OFFSETS = (0, 768, 1280, 1792, 2304)
PEER_HEADS = 8
PEER_KEYS = 128
PEER_TOPK = 16
PEER_HALF = 128
PEER_SEL = PEER_HEADS * PEER_TOPK
N_EXPERTS = PEER_KEYS * PEER_KEYS

OFF_Z = 0
OFF_XBC = OFF_Z + SSD_INNER
OFF_DT = OFF_XBC + SSD_CONV_CH
OFF_S5 = OFF_DT + SSD_HEADS
OFF_SC = OFF_S5 + S5_WIDTH
OFF_CF = OFF_SC + 3 * SC_WIDTH
OFF_GATE = OFF_CF + 2 * CF_WIDTH
IN_COLS = OFF_GATE + N_BRANCH * D_MODEL

LANE = 128
P_ZX = 0
P_S5 = 2048
P_SC = P_S5 + S5_WIDTH
P_CF = P_SC + 3 * SC_WIDTH
P_GATE = P_CF + 2 * CF_WIDTH
P_DT = P_GATE + N_BRANCH * D_MODEL
P_COLS = 9728
PROJ_TN = 512
PROJ_DTYPE = BF16

VMEM_LIMIT = 56 * 1024 * 1024


def _cparams(*sem):
    return pltpu.CompilerParams(dimension_semantics=sem, vmem_limit_bytes=VMEM_LIMIT)


def _dot(a, b):
    return jnp.dot(a, b, preferred_element_type=F32)


def _dot_nt(a, b):
    return lax.dot_general(a, b, (((1,), (1,)), ((), ())), preferred_element_type=F32)


def _rms(x, g):
    return x * lax.rsqrt(jnp.mean(x * x, axis=-1, keepdims=True) + EPS) * g


def _inproj_kernel(x_ref, g_ref, w_ref, o_ref, h_ref):
    @pl.when(pl.program_id(1) == 0)
    def _():
        h_ref[...] = _rms(x_ref[...], g_ref[...]).astype(BF16)

    o_ref[...] = _dot(h_ref[...], w_ref[...]).astype(o_ref.dtype)


def _inproj(x, g, w):
    t = x.shape[0]
    tm = min(2048, t)
    return pl.pallas_call(
        _inproj_kernel,
        grid=(t // tm, P_COLS // PROJ_TN),
        in_specs=[pl.BlockSpec((tm, D_MODEL), lambda i, j: (i, 0)),
                  pl.BlockSpec((1, D_MODEL), lambda i, j: (0, 0)),
                  pl.BlockSpec((D_MODEL, PROJ_TN), lambda i, j: (0, j))],
        out_specs=pl.BlockSpec((tm, PROJ_TN), lambda i, j: (i, j)),
        out_shape=jax.ShapeDtypeStruct((t, P_COLS), PROJ_DTYPE),
        scratch_shapes=[pltpu.VMEM((tm, D_MODEL), BF16)],
        compiler_params=_cparams("arbitrary", "arbitrary"),
        name="inproj",
    )(x, g, w)


def _split3(x):
    hi = x.astype(BF16)
    r = x - hi.astype(F32)
    mid = r.astype(BF16)
    lo = (r - mid.astype(F32)).astype(BF16)
    return hi, mid, lo


def _ssd_kernel(zx_ref, dt_ref, cw_ref, cb_ref, dtb_ref, alog_ref, dsk_ref, ng_ref, o_ref,
                xbuf, state, ybuf):
    L = SSD_CHUNK
    c = pl.program_id(1)

    @pl.when(c == 0)
    def _():
        xbuf[0:8, :] = jnp.zeros((8, SSD_CONV_CH), F32)
        state[...] = jnp.zeros(state.shape, F32)

    xbuf[8:8 + L, :] = zx_ref[:, SSD_INNER:SSD_INNER + SSD_CONV_CH].astype(F32)
    conv = cb_ref[...] + cw_ref[0:1, :] * xbuf[5:5 + L, :]
    for k in range(1, SSD_CONV):
        conv = conv + cw_ref[k:k + 1, :] * xbuf[5 + k:5 + k + L, :]
    xbuf[0:8, :] = xbuf[L:L + 8, :]
    xc = jax.nn.silu(conv)

    gn = SSD_GROUPS * SSD_STATE
    bm = xc[:, SSD_INNER:SSD_INNER + gn]
    cm = xc[:, SSD_INNER + gn:SSD_INNER + 2 * gn]
    bm_b = bm.astype(BF16)
    cm_b = cm.astype(BF16)
    bmT_b = bm.T.astype(BF16)

    dt = jax.nn.softplus(dt_ref[...].astype(F32) + dtb_ref[...])
    a = -jnp.exp(alog_ref[...])
    adt = dt * a

    row = lax.broadcasted_iota(jnp.int32, (L, L), 0)
    col = lax.broadcasted_iota(jnp.int32, (L, L), 1)
    tril = row >= col
    tri_b = jnp.where(tril, 1.0, 0.0).astype(BF16)
    hi, mid, lo = _split3(adt)
    a_cum = _dot(tri_b, hi) + _dot(tri_b, mid) + _dot(tri_b, lo)
    a_cum_t = a_cum.T

    def spread(parts, width):
        n = SSD_HEADS * width
        r = lax.broadcasted_iota(jnp.int32, (LANE, n), 0)
        q = lax.broadcasted_iota(jnp.int32, (LANE, n), 1) // width
        e = jnp.where(r == q, 1.0, 0.0).astype(BF16)
        return _dot(parts[0], e) + _dot(parts[1], e) + _dot(parts[2], e)

    ac3 = _split3(a_cum)
    dt_x = spread(_split3(dt), SSD_HEAD_DIM)
    ac_x = spread(ac3, SSD_HEAD_DIM)
    ac_w = spread(ac3, LANE)
    xs = xc[:, 0:SSD_INNER]
    xdt = xs * dt_x
    al_x = ac_x[L - 1:L, :]
    eac = jnp.exp(ac_x)
    w_b = (xdt * jnp.exp(al_x - ac_x)).astype(BF16)
    sdec = jnp.exp(al_x)
    xdt_b = xdt.astype(BF16)

    cm_gs = [cm_b[:, g * SSD_STATE:(g + 1) * SSD_STATE] for g in range(SSD_GROUPS)]
    bmT_gs = [bmT_b[g * SSD_STATE:(g + 1) * SSD_STATE, :] for g in range(SSD_GROUPS)]
    cbms = [_dot_nt(cm_gs[g], bm_b[:, g * SSD_STATE:(g + 1) * SSD_STATE]) for g in range(SSD_GROUPS)]
    first_y = lax.broadcasted_iota(jnp.int32, (L, LANE), 1) < SSD_HEAD_DIM
    first_s = lax.broadcasted_iota(jnp.int32, (SSD_STATE, LANE), 1) < SSD_HEAD_DIM

    for pair in range(SSD_HEADS // 2):
        cols = slice(pair * LANE, (pair + 1) * LANE)
        prev = state[pair]
        prev_b = prev.astype(BF16)
        ys, news = [], []
        for h in (2 * pair, 2 * pair + 1):
            g = h // SSD_HPG
            dec = jnp.exp(jnp.where(tril, ac_w[:, h * LANE:(h + 1) * LANE] - a_cum_t[h:h + 1, :], -jnp.inf))
            y = _dot((cbms[g] * dec).astype(BF16), xdt_b[:, cols])
            ys.append(y + _dot(cm_gs[g], prev_b) * eac[:, cols])
            news.append(_dot(bmT_gs[g], w_b[:, cols]))
        state[pair] = sdec[:, cols] * prev + jnp.where(first_s, news[0], news[1])
        ybuf[:, cols] = jnp.where(first_y, ys[0], ys[1])

    z = zx_ref[:, 0:SSD_INNER].astype(F32)
    y = ybuf[...] + dsk_ref[...] * xs
    o_ref[...] = _rms(y * jax.nn.silu(z), ng_ref[...])


def _ssd(proj, bsz, s, cw, cb, dtb, alog, dsk, ng):
    t = bsz * s
    nc = s // SSD_CHUNK
    L = SSD_CHUNK
    const = lambda b, c: (0, 0)
    return pl.pallas_call(
        _ssd_kernel,
        grid=(bsz, nc),
        in_specs=[pl.BlockSpec((L, 2048), lambda b, c: (b * nc + c, 0)),
                  pl.BlockSpec((L, LANE), lambda b, c: (b * nc + c, P_DT // LANE)),
                  pl.BlockSpec((SSD_CONV, SSD_CONV_CH), const),
                  pl.BlockSpec((1, SSD_CONV_CH), const),
                  pl.BlockSpec((1, LANE), const),
                  pl.BlockSpec((1, LANE), const),
                  pl.BlockSpec((1, SSD_INNER), const),
                  pl.BlockSpec((1, SSD_INNER), const)],
        out_specs=pl.BlockSpec((L, SSD_INNER), lambda b, c: (b * nc + c, 0)),
        out_shape=jax.ShapeDtypeStruct((t, SSD_INNER), F32),
        scratch_shapes=[pltpu.VMEM((L + 8, SSD_CONV_CH), F32),
                        pltpu.VMEM((SSD_HEADS // 2, SSD_STATE, 2 * SSD_HEAD_DIM), F32),
                        pltpu.VMEM((L, SSD_INNER), F32)],
        compiler_params=_cparams("arbitrary", "arbitrary"),
        name="ssd",
    )(proj, proj, cw, cb, dtb, alog, dsk, ng)


def _s5_disc_kernel(lre_ref, lim_ref, ls_ref, bre_ref, bim_ref, ore_ref, oim_ref, obre_ref, obim_ref):
    lr = lre_ref[...]
    li = lim_ref[...]
    step = jnp.exp(ls_ref[...])
    mag = jnp.exp(lr * step)
    pr = mag * jnp.cos(li * step)
    pi = mag * jnp.sin(li * step)
    ore_ref[...] = pr
    oim_ref[...] = pi
    qr = pr - 1.0
    den = lr * lr + li * li
    cr = (qr * lr + pi * li) / den
    ci = (pi * lr - qr * li) / den
    br = bre_ref[...]
    bi = bim_ref[...]
    obre_ref[...] = cr * br - ci * bi
    obim_ref[...] = cr * bi + ci * br


def _s5_discretise(lam_re, lam_im, log_step, b_re, b_im):
    n = S5_STATE * S5_GROUP
    rep = lambda a: jnp.repeat(a, S5_GROUP, axis=-1)
    args = (rep(lam_re), rep(lam_im), jnp.broadcast_to(log_step[:, None], (S5_GROUPS, n)),
            b_re.reshape(S5_GROUPS, n), b_im.reshape(S5_GROUPS, n))
    outs = pl.pallas_call(
        _s5_disc_kernel,
        out_shape=[jax.ShapeDtypeStruct((S5_GROUPS, n), F32)] * 4,
        name="s5_disc",
    )(*args)
    lamb_re, lamb_im, bb_re, bb_im = outs
    lamb_re = lamb_re[:, ::S5_GROUP].reshape(1, S5_LANES)
    lamb_im = lamb_im[:, ::S5_GROUP].reshape(1, S5_LANES)
    return lamb_re, lamb_im, bb_re.reshape(S5_GROUPS, S5_STATE, S5_GROUP), bb_im.reshape(S5_GROUPS, S5_STATE, S5_GROUP)


def _block_diag(blocks):
    g, r, c = blocks.shape
    eye = jnp.eye(g, dtype=blocks.dtype)
    return (blocks[:, :, None, :] * eye[:, None, :, None]).reshape(g * r, g * c)


S5_COLS = 512
S5_BSPLIT = 2


def _s5_kernel(u_ref, lre_ref, lim_ref, bre_ref, bim_ref, cre_ref, cimn_ref, d_ref, wglu_ref, o_ref,
               hre, him, sre, sim, tbuf):
    nb, steps, _ = u_ref.shape
    rows = nb * steps
    nslab = S5_WIDTH // LANE

    @pl.when(pl.program_id(0) == 0)
    def _():
        sre[...] = jnp.zeros(sre.shape, F32)
        sim[...] = jnp.zeros(sim.shape, F32)

    for b in range(nb):
        for cs in range(nslab):
            tbuf[pl.ds(cs * rows + b, steps, stride=nb), :] = u_ref[b, :, cs * LANE:(cs + 1) * LANE].astype(F32)
    u = jnp.concatenate([tbuf[cs * rows:(cs + 1) * rows, :] for cs in range(nslab)], axis=1)
    ub = u.astype(BF16)
    kin = S5_WIDTH // S5_BSPLIT
    kst = S5_LANES // S5_BSPLIT
    for kb in range(S5_BSPLIT):
        hre[:, kb * kst:(kb + 1) * kst] = _dot(ub[:, kb * kin:(kb + 1) * kin],
                                               bre_ref[kb * kin:(kb + 1) * kin, kb * kst:(kb + 1) * kst])
        him[:, kb * kst:(kb + 1) * kst] = _dot(ub[:, kb * kin:(kb + 1) * kin],
                                               bim_ref[kb * kin:(kb + 1) * kin, kb * kst:(kb + 1) * kst])

    for cb in range(S5_LANES // S5_COLS):
        cols = slice(cb * S5_COLS, (cb + 1) * S5_COLS)
        lr = jnp.broadcast_to(lre_ref[:, cols], (8, S5_COLS))
        li = jnp.broadcast_to(lim_ref[:, cols], (8, S5_COLS))

        def body(t, carry, cols=cols, lr=lr, li=li):
            hr, hi = carry
            r0 = pl.multiple_of(t * 8, 8)
            nr = lr * hr - li * hi + hre[pl.ds(r0, 8), cols]
            ni = lr * hi + li * hr + him[pl.ds(r0, 8), cols]
            hre[pl.ds(r0, 8), cols] = nr
            him[pl.ds(r0, 8), cols] = ni
            return nr, ni

        hr, hi = lax.fori_loop(0, steps, body, (sre[:, cols], sim[:, cols]), unroll=8)
        sre[:, cols] = hr
        sim[:, cols] = hi

    kst = S5_LANES // nslab
    ys = []
    for cs in range(nslab):
        st = slice(cs * kst, (cs + 1) * kst)
        lanes = slice(cs * LANE, (cs + 1) * LANE)
        ys.append(_dot(hre[:, st].astype(BF16), cre_ref[st, lanes]) + _dot(him[:, st].astype(BF16), cimn_ref[st, lanes]))
    y = jnp.concatenate(ys, axis=1) + d_ref[...] * u
    y = jax.nn.gelu(y)
    y = y * jax.nn.sigmoid(_dot(y.astype(BF16), wglu_ref[...]))
    for cs in range(nslab):
        tbuf[cs * rows:(cs + 1) * rows, :] = y[:, cs * LANE:(cs + 1) * LANE]
    for b in range(nb):
        for cs in range(nslab):
            o_ref[b, :, cs * LANE:(cs + 1) * LANE] = tbuf[pl.ds(cs * rows + b, steps, stride=nb), :]


S5_STEPS = 64


def _s5(proj3, lamb_re, lamb_im, bmat_re, bmat_im, cmat_re, cmat_imn, d, wglu):
    bsz, s, _ = proj3.shape
    steps = min(S5_STEPS, s)
    rows = bsz * steps
    const = lambda i: (0, 0)
    return pl.pallas_call(
        _s5_kernel,
        grid=(s // steps,),
        in_specs=[pl.BlockSpec((bsz, steps, S5_WIDTH), lambda i: (0, i, P_S5 // S5_WIDTH)),
                  pl.BlockSpec((1, S5_LANES), const),
                  pl.BlockSpec((1, S5_LANES), const),
                  pl.BlockSpec((S5_WIDTH, S5_LANES), const),
                  pl.BlockSpec((S5_WIDTH, S5_LANES), const),
                  pl.BlockSpec((S5_LANES, S5_WIDTH), const),
                  pl.BlockSpec((S5_LANES, S5_WIDTH), const),
                  pl.BlockSpec((1, S5_WIDTH), const),
                  pl.BlockSpec((S5_WIDTH, S5_WIDTH), const)],
        out_specs=pl.BlockSpec((bsz, steps, S5_WIDTH), lambda i: (0, i, 0)),
        out_shape=jax.ShapeDtypeStruct((bsz, s, S5_WIDTH), F32),
        scratch_shapes=[pltpu.VMEM((rows, S5_LANES), F32), pltpu.VMEM((rows, S5_LANES), F32),
                        pltpu.VMEM((8, S5_LANES), F32), pltpu.VMEM((8, S5_LANES), F32),
                        pltpu.VMEM((S5_WIDTH // LANE * rows, LANE), F32)],
        compiler_params=_cparams("arbitrary"),
        name="s5",
    )(proj3, lamb_re, lamb_im, bmat_re, bmat_im, cmat_re, cmat_imn, d, wglu)


CONV_ROWS = 64


def _conv_kernel(scb_ref, scc_ref, sch_ref, cfa_ref, cfg_ref, scw_ref, cfw_ref, lng_ref, lnb_ref, oc_ref, od_ref,
                 chbuf, vbuf, ycbuf, ydbuf):
    nb, steps, _ = scb_ref.shape
    rows = nb * steps
    nslab = SC_WIDTH // LANE
    sc_halo = (SC_CONV - 1) * nb
    cf_halo = (CF_CONV - 1) * nb
    sc_pitch = sc_halo + rows
    cf_pitch = cf_halo + rows

    @pl.when(pl.program_id(0) == 0)
    def _():
        for cs in range(nslab):
            chbuf[cs * sc_pitch:cs * sc_pitch + sc_halo, :] = jnp.zeros((sc_halo, LANE), F32)
            vbuf[cs * cf_pitch:cs * cf_pitch + cf_halo, :] = jnp.zeros((cf_halo, LANE), F32)

    for b in range(nb):
        ch = scc_ref[b].astype(F32) * sch_ref[b].astype(F32)
        v = cfa_ref[b].astype(F32) * jax.nn.sigmoid(cfg_ref[b].astype(F32))
        for cs in range(nslab):
            lanes = slice(cs * LANE, (cs + 1) * LANE)
            chbuf[pl.ds(cs * sc_pitch + sc_halo + b, steps, stride=nb), :] = ch[:, lanes]
            vbuf[pl.ds(cs * cf_pitch + cf_halo + b, steps, stride=nb), :] = v[:, lanes]

    def block(rb, carry):
        r0 = pl.multiple_of(rb * CONV_ROWS, CONV_ROWS)
        for cs in range(nslab):
            lanes = slice(cs * LANE, (cs + 1) * LANE)
            acc = scw_ref[0:1, lanes] * chbuf[pl.ds(cs * sc_pitch + r0, CONV_ROWS), :]
            for k in range(1, SC_CONV):
                acc = acc + scw_ref[k:k + 1, lanes] * chbuf[pl.ds(cs * sc_pitch + k * nb + r0, CONV_ROWS), :]
            ycbuf[pl.ds(cs * rows + r0, CONV_ROWS), :] = acc
            acc = cfw_ref[0:1, lanes] * vbuf[pl.ds(cs * cf_pitch + r0, CONV_ROWS), :]
            for k in range(1, CF_CONV):
                acc = acc + cfw_ref[k:k + 1, lanes] * vbuf[pl.ds(cs * cf_pitch + k * nb + r0, CONV_ROWS), :]
            ydbuf[pl.ds(cs * rows + r0, CONV_ROWS), :] = acc
        return carry

    lax.fori_loop(0, rows // CONV_ROWS, block, 0)

    y = jnp.concatenate([ydbuf[cs * rows:(cs + 1) * rows, :] for cs in range(nslab)], axis=1)
    mu = jnp.mean(y, axis=-1, keepdims=True)
    var = jnp.mean(jnp.square(y - mu), axis=-1, keepdims=True)
    y = jax.nn.silu((y - mu) * lax.rsqrt(var + EPS) * lng_ref[...] + lnb_ref[...])
    for cs in range(nslab):
        ydbuf[cs * rows:(cs + 1) * rows, :] = y[:, cs * LANE:(cs + 1) * LANE]

    for b in range(nb):
        for cs in range(nslab):
            lanes = slice(cs * LANE, (cs + 1) * LANE)
            oc_ref[b, :, lanes] = scb_ref[b, :, lanes].astype(F32) * ycbuf[pl.ds(cs * rows + b, steps, stride=nb), :]
            od_ref[b, :, lanes] = ydbuf[pl.ds(cs * rows + b, steps, stride=nb), :]

    for cs in range(nslab):
        chbuf[cs * sc_pitch:cs * sc_pitch + sc_halo, :] = chbuf[cs * sc_pitch + rows:cs * sc_pitch + rows + sc_halo, :]
        vbuf[cs * cf_pitch:cs * cf_pitch + cf_halo, :] = vbuf[cs * cf_pitch + rows:cs * cf_pitch + rows + cf_halo, :]


CONV_STEPS = 64


def _convs(proj3, scw, cfw, lng, lnb):
    bsz, s, _ = proj3.shape
    steps = min(CONV_STEPS, s)
    rows = bsz * steps
    nslab = SC_WIDTH // LANE
    const = lambda i: (0, 0)
    blk = lambda col: pl.BlockSpec((bsz, steps, SC_WIDTH), lambda i: (0, i, col))
    out = pl.BlockSpec((bsz, steps, SC_WIDTH), lambda i: (0, i, 0))
    return pl.pallas_call(
        _conv_kernel,
        grid=(s // steps,),
        in_specs=[blk(P_SC // SC_WIDTH), blk(P_SC // SC_WIDTH + 1), blk(P_SC // SC_WIDTH + 2),
                  blk(P_CF // CF_WIDTH), blk(P_CF // CF_WIDTH + 1),
                  pl.BlockSpec((SC_CONV, SC_WIDTH), const),
                  pl.BlockSpec((CF_CONV, CF_WIDTH), const),
                  pl.BlockSpec((1, CF_WIDTH), const),
                  pl.BlockSpec((1, CF_WIDTH), const)],
        out_specs=[out, out],
        out_shape=[jax.ShapeDtypeStruct((bsz, s, SC_WIDTH), F32), jax.ShapeDtypeStruct((bsz, s, CF_WIDTH), F32)],
        scratch_shapes=[pltpu.VMEM((nslab * ((SC_CONV - 1) * bsz + rows), LANE), F32),
                        pltpu.VMEM((nslab * ((CF_CONV - 1) * bsz + rows), LANE), F32),
                        pltpu.VMEM((nslab * rows, LANE), F32),
                        pltpu.VMEM((nslab * rows, LANE), F32)],
        compiler_params=_cparams("arbitrary"),
        name="convs",
    )(proj3, proj3, proj3, proj3, proj3, scw, cfw, lng, lnb)


def _merge_kernel(x_ref, ya_ref, yb_ref, yc_ref, yd_ref, ga_ref, gb_ref, gc_ref, gd_ref, wb_ref, wout_ref, g2_ref,
                  wq_ref, xo_ref, h2_ref, q_ref):
    merged = None
    branches = ((ya_ref, ga_ref), (yb_ref, gb_ref), (yc_ref, gc_ref), (yd_ref, gd_ref))
    for i, (y_ref, gate_ref) in enumerate(branches):
        lo, hi = BRANCH_OFFSETS[i], BRANCH_OFFSETS[i + 1]
        t = _dot(y_ref[...].astype(BF16), wb_ref[lo:hi, :])
        t = jax.nn.sigmoid(gate_ref[...].astype(F32)) * t
        merged = t if merged is None else merged + t
    xn = x_ref[...] + _dot(merged.astype(BF16), wout_ref[...])
    xo_ref[...] = xn
    h2 = _rms(xn, g2_ref[...]).astype(BF16)
    h2_ref[...] = h2
    q_ref[...] = _dot(h2, wq_ref[...]).astype(BF16)


def _merge(x, ya, yb, yc, yd, proj, wb, wout, g2, wq):
    t = x.shape[0]
    tm = min(512, t)
    nq = wq.shape[1]
    row = lambda i: (i, 0)
    const = lambda i: (0, 0)
    return pl.pallas_call(
        _merge_kernel,
        grid=(t // tm,),
        in_specs=[pl.BlockSpec((tm, D_MODEL), row),
                  pl.BlockSpec((tm, SSD_INNER), row),
                  pl.BlockSpec((tm, S5_WIDTH), row),
                  pl.BlockSpec((tm, SC_WIDTH), row),
                  pl.BlockSpec((tm, CF_WIDTH), row),
                  pl.BlockSpec((tm, D_MODEL), lambda i: (i, P_GATE // D_MODEL)),
                  pl.BlockSpec((tm, D_MODEL), lambda i: (i, P_GATE // D_MODEL + 1)),
                  pl.BlockSpec((tm, D_MODEL), lambda i: (i, P_GATE // D_MODEL + 2)),
                  pl.BlockSpec((tm, D_MODEL), lambda i: (i, P_GATE // D_MODEL + 3)),
                  pl.BlockSpec((BRANCH_OFFSETS[-1], D_MODEL), const),
                  pl.BlockSpec((D_MODEL, D_MODEL), const),
                  pl.BlockSpec((1, D_MODEL), const),
                  pl.BlockSpec((D_MODEL, nq), const)],
        out_specs=[pl.BlockSpec((tm, D_MODEL), row), pl.BlockSpec((tm, D_MODEL), row), pl.BlockSpec((tm, nq), row)],
        out_shape=[jax.ShapeDtypeStruct((t, D_MODEL), F32), jax.ShapeDtypeStruct((t, D_MODEL), BF16),
                   jax.ShapeDtypeStruct((t, nq), BF16)],
        compiler_params=_cparams("arbitrary"),
        name="merge",
    )(x, ya, yb, yc, yd, proj, proj, proj, proj, wb, wout, g2, wq)


NEG_INF = float("-inf")
CAND_B0, CAND_B1, CAND_A0, CAND_A1, CAND_ROWS = 0, 16, 24, 40, 72
CAND_LAST_A = 4


def _top16(x, vals_ref, idx_ref, ids):
    big = jnp.float32(1e9)
    for r in range(PEER_TOPK):
        m = jnp.max(x, axis=0, keepdims=True)
        sel = jnp.min(jnp.where(x == m, ids, big), axis=0, keepdims=True)
        vals_ref[r:r + 1, :] = m
        idx_ref[r:r + 1, :] = sel
        x = jnp.where(ids == sel, NEG_INF, x)


def _route_kernel(q_ref, k1_ref, k2_ref, oi_ref, oj_ref, og_ref,
                  v1, i1, v2, i2, cand, best, pos, si, sj, sg):
    tn = q_ref.shape[0]
    key_ids = lax.broadcasted_iota(jnp.int32, (PEER_KEYS, tn), 0).astype(F32)
    r = lax.broadcasted_iota(jnp.int32, (CAND_ROWS, tn), 0)
    ra = r - CAND_A1
    b_in_a = jnp.where(r < CAND_A1, r - CAND_A0, ra & 7)
    pos_ids = jnp.where(r < CAND_B1, r * PEER_TOPK,
                        jnp.where(r < CAND_A0, (r - CAND_B1) * PEER_TOPK + 1,
                                  jnp.where(r < CAND_A1, b_in_a, (1 + (ra >> 3)) * PEER_TOPK + b_in_a))).astype(F32)
    cand_dup = jnp.logical_and(r >= CAND_A0, b_in_a < 2)

    def head(h, carry):
        c0 = pl.multiple_of(h * 2 * PEER_HALF, 2 * PEER_HALF)
        q1 = q_ref[:, pl.ds(c0, PEER_HALF)]
        q2 = q_ref[:, pl.ds(c0 + PEER_HALF, PEER_HALF)]
        _top16(_dot_nt(k1_ref[...], q1), v1, i1, key_ids)
        _top16(_dot_nt(k2_ref[...], q2), v2, i2, key_ids)
        v1a = v1[...]
        v2a = v2[...]
        cand[CAND_B0:CAND_B1, :] = v1a + v2a[0:1, :]
        cand[CAND_B1:CAND_A0, :] = v1a[0:8, :] + v2a[1:2, :]
        cand[CAND_A0:CAND_A1, :] = v1a[0:1, :] + v2a
        for a in range(1, CAND_LAST_A + 1):
            o = CAND_A1 + (a - 1) * 8
            cand[o:o + 8, :] = v1a[a:a + 1, :] + v2a[0:8, :]
        _top16(jnp.where(cand_dup, NEG_INF, cand[...]), best, pos, pos_ids)
        p = pos[...]
        a_k = jnp.floor(p * (1.0 / PEER_TOPK))
        b_k = p - a_k * PEER_TOPK
        isel = jnp.zeros((PEER_TOPK, tn), F32)
        jsel = jnp.zeros((PEER_TOPK, tn), F32)
        for a in range(PEER_TOPK):
            isel = jnp.where(a_k == a, i1[a:a + 1, :], isel)
            jsel = jnp.where(b_k == a, i2[a:a + 1, :], jsel)
        r0 = pl.multiple_of(h * PEER_TOPK, PEER_TOPK)
        si[pl.ds(r0, PEER_TOPK), :] = isel
        sj[pl.ds(r0, PEER_TOPK), :] = jsel
        sg[pl.ds(r0, PEER_TOPK), :] = jax.nn.softmax(best[...], axis=0)
        return carry

    lax.fori_loop(0, PEER_HEADS, head, 0)
    oi_ref[...] = si[...].T
    oj_ref[...] = sj[...].T
    og_ref[...] = sg[...].T


def _route(q, k1, k2):
    t = q.shape[0]
    tn = min(1024, t)
    row = lambda i: (i, 0)
    const = lambda i: (0, 0)
    vm = lambda n: pltpu.VMEM((n, tn), F32)
    return pl.pallas_call(
        _route_kernel,
        grid=(t // tn,),
        in_specs=[pl.BlockSpec((tn, q.shape[1]), row),
                  pl.BlockSpec((PEER_KEYS, PEER_HALF), const),
                  pl.BlockSpec((PEER_KEYS, PEER_HALF), const)],
        out_specs=[pl.BlockSpec((tn, PEER_SEL), row)] * 3,
        out_shape=[jax.ShapeDtypeStruct((t, PEER_SEL), F32)] * 3,
        scratch_shapes=[vm(PEER_TOPK), vm(PEER_TOPK), vm(PEER_TOPK), vm(PEER_TOPK), vm(CAND_ROWS),
                        vm(PEER_TOPK), vm(PEER_TOPK), vm(PEER_SEL), vm(PEER_SEL), vm(PEER_SEL)],
        compiler_params=_cparams("arbitrary"),
        name="route",
    )(q, k1, k2)


PEER_TE = 2048
GS_PITCH = 136
GS_TOK = 32


def _peer_kernel(h_ref, ii_ref, jj_ref, gg_ref, x_ref, u_ref, v_ref, fg_ref, o_ref, gmat, acc, gs, *, final):
    tm = h_ref.shape[0]
    e = pl.program_id(1)

    @pl.when(e == 0)
    def _():
        acc[...] = jnp.zeros(acc.shape, F32)
        ids = lax.broadcasted_iota(jnp.int32, (PEER_KEYS, PEER_SEL), 0).astype(F32)

        def group(gi, carry):
            t0 = pl.multiple_of(gi * GS_TOK, GS_TOK)
            for tt in range(GS_TOK):
                irow = ii_ref[pl.ds(t0 + tt, 1), :]
                jrow = jj_ref[pl.ds(t0 + tt, 1), :]
                grow = gg_ref[pl.ds(t0 + tt, 1), :]
                pt = jnp.where(ids == irow, 1.0, 0.0).astype(BF16)
                qt = jnp.where(ids == jrow, grow, 0.0).astype(BF16)
                gs[tt * GS_PITCH:tt * GS_PITCH + PEER_KEYS, :] = _dot_nt(pt, qt)

            for i in range(PEER_KEYS):
                blk = gs[pl.ds(i, GS_TOK, stride=GS_PITCH), :]
                gmat[pl.ds(t0, GS_TOK), i * PEER_KEYS:(i + 1) * PEER_KEYS] = blk.astype(BF16)
            return carry

        lax.fori_loop(0, tm // GS_TOK, group, 0)

    s = _dot_nt(h_ref[...], u_ref[...])
    w = gmat[:, pl.ds(pl.multiple_of(e * PEER_TE, PEER_TE), PEER_TE)]
    a = (jax.nn.gelu(s) * w.astype(F32)).astype(BF16)
    acc[...] += _dot(a, v_ref[...])

    @pl.when(e == pl.num_programs(1) - 1)
    def _():
        xn = x_ref[...] + acc[...]
        o_ref[...] = _rms(xn, fg_ref[...]) if final else xn


def _peer(h2, ii, jj, gg, x, u, v, l, fg, final):
    t = h2.shape[0]
    tm = min(512, t)
    row = lambda i, e: (i, 0)
    return pl.pallas_call(
        functools.partial(_peer_kernel, final=final),
        grid=(t // tm, N_EXPERTS // PEER_TE),
        in_specs=[pl.BlockSpec((tm, D_MODEL), row),
                  pl.BlockSpec((tm, PEER_SEL), row),
                  pl.BlockSpec((tm, PEER_SEL), row),
                  pl.BlockSpec((tm, PEER_SEL), row),
                  pl.BlockSpec((tm, D_MODEL), row),
                  pl.BlockSpec((None, PEER_TE, D_MODEL), lambda i, e: (l, e, 0)),
                  pl.BlockSpec((None, PEER_TE, D_MODEL), lambda i, e: (l, e, 0)),
                  pl.BlockSpec((1, D_MODEL), lambda i, e: (0, 0))],
        out_specs=pl.BlockSpec((tm, D_MODEL), row),
        out_shape=jax.ShapeDtypeStruct((t, D_MODEL), F32),
        scratch_shapes=[pltpu.VMEM((tm, N_EXPERTS), BF16),
                        pltpu.VMEM((tm, D_MODEL), F32),
                        pltpu.VMEM((GS_TOK * GS_PITCH, PEER_KEYS), F32)],
        compiler_params=_cparams("arbitrary", "arbitrary"),
        name="peer",
    )(h2, ii, jj, gg, x, u, v, fg)


def _pad_lanes(v, n=LANE):
    return jnp.pad(v, (0, n - v.shape[0])).reshape(1, n)


N_ZX_TILES = OFF_DT // PROJ_TN
DT_TILE = P_DT // PROJ_TN


def _relayout_kernel(a_ref, b_ref, o_ref):
    j = pl.program_id(0)

    @pl.when(j < N_ZX_TILES)
    def _():
        o_ref[...] = a_ref[...].astype(BF16)

    @pl.when(jnp.logical_and(j >= N_ZX_TILES, j < DT_TILE))
    def _():
        w = jnp.concatenate([a_ref[...], b_ref[...]], axis=1)
        o_ref[...] = w[:, SSD_HEADS:SSD_HEADS + PROJ_TN].astype(BF16)

    @pl.when(j == DT_TILE)
    def _():
        lane = lax.broadcasted_iota(jnp.int32, a_ref.shape, 1)
        o_ref[...] = jnp.where(lane < SSD_HEADS, a_ref[...], 0.0).astype(BF16)


def _relayout_w_in(w, l):
    assert OFF_DT == P_S5 and IN_COLS - OFF_S5 == P_DT - P_S5 and OFF_DT % PROJ_TN == 0
    a_idx = lambda j: (l, 0, jnp.where(j == DT_TILE, N_ZX_TILES, j))
    b_idx = lambda j: (l, 0, jnp.clip(j + 1, N_ZX_TILES + 1, DT_TILE) * (PROJ_TN // LANE))
    return pl.pallas_call(
        _relayout_kernel,
        grid=(DT_TILE + 1,),
        in_specs=[pl.BlockSpec((None, D_MODEL, PROJ_TN), a_idx), pl.BlockSpec((None, D_MODEL, LANE), b_idx)],
        out_specs=pl.BlockSpec((D_MODEL, PROJ_TN), lambda j: (0, j)),
        out_shape=jax.ShapeDtypeStruct((D_MODEL, P_COLS), BF16),
        compiler_params=_cparams("arbitrary"),
        name="w_in_relayout",
    )(w, w)


def kernel(x, norm1_g, w_in, ssd_conv_w, ssd_conv_b, ssd_dt_bias, ssd_a_log, ssd_d, ssd_norm_g, s5_lam_re, s5_lam_im, s5_log_step, s5_b_re, s5_b_im, s5_c_re, s5_c_im, s5_d, s5_w_glu, sc_conv_w, cf_conv_w, cf_ln_g, cf_ln_b, w_branch, w_out, norm2_g, peer_w_query, peer_sub_keys, peer_u, peer_v, final_norm_g):
    bsz, s, d = x.shape
    t = bsz * s
    depth = w_in.shape[0]
    xt = x.reshape(t, d)
    u_b = peer_u.astype(BF16)
    v_b = peer_v.astype(BF16)
    for l in range(depth):
        proj = _inproj(xt, norm1_g[l].reshape(1, d), _relayout_w_in(w_in, l))

        ya = _ssd(proj, bsz, s, ssd_conv_w[l], ssd_conv_b[l].reshape(1, -1), _pad_lanes(ssd_dt_bias[l]),
                  _pad_lanes(ssd_a_log[l]), jnp.repeat(ssd_d[l], SSD_HEAD_DIM).reshape(1, -1),
                  ssd_norm_g[l].reshape(1, -1))

        lamb_re, lamb_im, bb_re, bb_im = _s5_discretise(s5_lam_re[l], s5_lam_im[l], s5_log_step[l],
                                                        s5_b_re[l], s5_b_im[l])
        bmat_re = _block_diag(bb_re.transpose(0, 2, 1)).astype(BF16)
        bmat_im = _block_diag(bb_im.transpose(0, 2, 1)).astype(BF16)
        cmat_re = _block_diag(s5_c_re[l].transpose(0, 2, 1)).astype(BF16)
        cmat_imn = _block_diag(-s5_c_im[l].transpose(0, 2, 1)).astype(BF16)
        proj3 = proj.reshape(bsz, s, P_COLS)
        yb = _s5(proj3, lamb_re, lamb_im, bmat_re, bmat_im, cmat_re, cmat_imn,
                 s5_d[l].reshape(1, -1), s5_w_glu[l].astype(BF16)).reshape(t, S5_WIDTH)

        yc, yd = _convs(proj3, sc_conv_w[l], cf_conv_w[l], cf_ln_g[l].reshape(1, -1), cf_ln_b[l].reshape(1, -1))
        yc = yc.reshape(t, SC_WIDTH)
        yd = yd.reshape(t, CF_WIDTH)

        xt, h2, q = _merge(xt, ya, yb, yc, yd, proj, w_branch[l].astype(BF16), w_out[l].astype(BF16),
                           norm2_g[l].reshape(1, d), peer_w_query[l].astype(BF16))

        ii, jj, gg = _route(q, peer_sub_keys[l, 0].astype(BF16), peer_sub_keys[l, 1].astype(BF16))
        xt = _peer(h2, ii, jj, gg, xt, u_b, v_b, l, final_norm_g.reshape(1, d), final=(l == depth - 1))
    return xt.reshape(bsz, s, d)
```

```python
import functools

import jax
import jax.numpy as jnp
from jax import lax
from jax.experimental import pallas as pl
from jax.experimental.pallas import tpu as pltpu

F32 = jnp.float32
BF16 = jnp.bfloat16
EPS = 1e-6

D_MODEL = 1024
SSD_HEADS = 12
SSD_HEAD_DIM = 64
SSD_INNER = SSD_HEADS * SSD_HEAD_DIM
SSD_GROUPS = 4
SSD_HPG = SSD_HEADS // SSD_GROUPS
SSD_STATE = 64
SSD_CONV = 4
SSD_CHUNK = 128
SSD_CONV_CH = SSD_INNER + 2 * SSD_GROUPS * SSD_STATE
S5_GROUP = 16
S5_WIDTH = 512
S5_GROUPS = S5_WIDTH // S5_GROUP
S5_STATE = 64
S5_LANES = S5_GROUPS * S5_STATE
SC_WIDTH = 512
SC_CONV = 3
CF_WIDTH = 512
CF_CONV = 31
N_BRANCH = 4
BRANCH_OFFSETS = (0, 768, 1280, 1792, 2304)
PEER_HEADS = 8
PEER_KEYS = 128
PEER_TOPK = 16
PEER_HALF = 128
PEER_SEL = PEER_HEADS * PEER_TOPK
N_EXPERTS = PEER_KEYS * PEER_KEYS

OFF_Z = 0
OFF_XBC = OFF_Z + SSD_INNER
OFF_DT = OFF_XBC + SSD_CONV_CH
OFF_S5 = OFF_DT + SSD_HEADS
OFF_SC = OFF_S5 + S5_WIDTH
OFF_CF = OFF_SC + 3 * SC_WIDTH
OFF_GATE = OFF_CF + 2 * CF_WIDTH
IN_COLS = OFF_GATE + N_BRANCH * D_MODEL

LANE = 128
P_ZX = 0
P_S5 = 2048
P_SC = P_S5 + S5_WIDTH
P_CF = P_SC + 3 * SC_WIDTH
P_GATE = P_CF + 2 * CF_WIDTH
P_DT = P_GATE + N_BRANCH * D_MODEL
P_COLS = 9728
PROJ_TN = 512
PROJ_DTYPE = BF16

VMEM_LIMIT = 56 * 1024 * 1024


def _cparams(*sem):
    return pltpu.CompilerParams(dimension_semantics=sem, vmem_limit_bytes=VMEM_LIMIT)


def _dot(a, b):
    return jnp.dot(a, b, preferred_element_type=F32)


def _dot_nt(a, b):
    return lax.dot_general(a, b, (((1,), (1,)), ((), ())), preferred_element_type=F32)


def _rms(x, g):
    return x * lax.rsqrt(jnp.mean(x * x, axis=-1, keepdims=True) + EPS) * g


def _inproj_kernel(x_ref, g_ref, w_ref, o_ref, h_ref):
    @pl.when(pl.program_id(1) == 0)
    def _():
        h_ref[...] = _rms(x_ref[...], g_ref[...]).astype(BF16)

    o_ref[...] = _dot(h_ref[...], w_ref[...]).astype(o_ref.dtype)


def _inproj(x, g, w):
    t = x.shape[0]
    tm = min(2048, t)
    return pl.pallas_call(
        _inproj_kernel,
        grid=(t // tm, P_COLS // PROJ_TN),
        in_specs=[pl.BlockSpec((tm, D_MODEL), lambda i, j: (i, 0)),
                  pl.BlockSpec((1, D_MODEL), lambda i, j: (0, 0)),
                  pl.BlockSpec((D_MODEL, PROJ_TN), lambda i, j: (0, j))],
        out_specs=pl.BlockSpec((tm, PROJ_TN), lambda i, j: (i, j)),
        out_shape=jax.ShapeDtypeStruct((t, P_COLS), PROJ_DTYPE),
        scratch_shapes=[pltpu.VMEM((tm, D_MODEL), BF16)],
        compiler_params=_cparams("arbitrary", "arbitrary"),
        name="inproj",
    )(x, g, w)


def _split3(x):
    hi = x.astype(BF16)
    r = x - hi.astype(F32)
    mid = r.astype(BF16)
    lo = (r - mid.astype(F32)).astype(BF16)
    return hi, mid, lo


def _ssd_kernel(zx_ref, dt_ref, cw_ref, cb_ref, dtb_ref, alog_ref, dsk_ref, ng_ref, o_ref,
                xbuf, state, ybuf):
    @pl.when(pl.program_id(1) == 0)
    def _():
        xbuf[0:8, :] = jnp.zeros((8, SSD_CONV_CH), F32)
        state[...] = jnp.zeros(state.shape, F32)

    for k in range(SSD_PER_STEP):
        rows = pl.ds(k * SSD_CHUNK, SSD_CHUNK)
        _ssd_chunk(zx_ref.at[rows], dt_ref.at[rows], cw_ref, cb_ref, dtb_ref, alog_ref, dsk_ref, ng_ref,
                   o_ref.at[rows], xbuf, state, ybuf)


SSD_PER_STEP = 2


def _ssd_chunk(zx_ref, dt_ref, cw_ref, cb_ref, dtb_ref, alog_ref, dsk_ref, ng_ref, o_ref, xbuf, state, ybuf):
    L = SSD_CHUNK
    xbuf[8:8 + L, :] = zx_ref[:, SSD_INNER:SSD_INNER + SSD_CONV_CH].astype(F32)
    conv = cb_ref[...] + cw_ref[0:1, :] * xbuf[5:5 + L, :]
    for k in range(1, SSD_CONV):
        conv = conv + cw_ref[k:k + 1, :] * xbuf[5 + k:5 + k + L, :]
    xbuf[0:8, :] = xbuf[L:L + 8, :]
    xc = jax.nn.silu(conv)

    gn = SSD_GROUPS * SSD_STATE
    bm = xc[:, SSD_INNER:SSD_INNER + gn]
    cm = xc[:, SSD_INNER + gn:SSD_INNER + 2 * gn]
    bm_b = bm.astype(BF16)
    cm_b = cm.astype(BF16)
    bmT_b = bm.T.astype(BF16)

    dt = jax.nn.softplus(dt_ref[...].astype(F32) + dtb_ref[...])
    a = -jnp.exp(alog_ref[...])
    adt = dt * a

    row = lax.broadcasted_iota(jnp.int32, (L, L), 0)
    col = lax.broadcasted_iota(jnp.int32, (L, L), 1)
    tril = row >= col
    tri_b = jnp.where(tril, 1.0, 0.0).astype(BF16)
    hi, mid, lo = _split3(adt)
    a_cum = _dot(tri_b, hi) + _dot(tri_b, mid) + _dot(tri_b, lo)
    a_cum_t = a_cum.T

    def spread(parts, width):
        n = SSD_HEADS * width
        r = lax.broadcasted_iota(jnp.int32, (LANE, n), 0)
        q = lax.broadcasted_iota(jnp.int32, (LANE, n), 1) // width
        e = jnp.where(r == q, 1.0, 0.0).astype(BF16)
        return _dot(parts[0], e) + _dot(parts[1], e) + _dot(parts[2], e)

    ac3 = _split3(a_cum)
    dt_x = spread(_split3(dt), SSD_HEAD_DIM)
    ac_x = spread(ac3, SSD_HEAD_DIM)
    ac_w = spread(ac3, LANE)
    xs = xc[:, 0:SSD_INNER]
    xdt = xs * dt_x
    al_x = ac_x[L - 1:L, :]
    eac = jnp.exp(ac_x)
    w_b = (xdt * jnp.exp(al_x - ac_x)).astype(BF16)
    sdec = jnp.exp(al_x)
    xdt_b = xdt.astype(BF16)

    cm_gs = [cm_b[:, g * SSD_STATE:(g + 1) * SSD_STATE] for g in range(SSD_GROUPS)]
    bmT_gs = [bmT_b[g * SSD_STATE:(g + 1) * SSD_STATE, :] for g in range(SSD_GROUPS)]
    cbms = [_dot_nt(cm_gs[g], bm_b[:, g * SSD_STATE:(g + 1) * SSD_STATE]) for g in range(SSD_GROUPS)]
    first_y = lax.broadcasted_iota(jnp.int32, (L, LANE), 1) < SSD_HEAD_DIM
    first_s = lax.broadcasted_iota(jnp.int32, (SSD_STATE, LANE), 1) < SSD_HEAD_DIM

    for pair in range(SSD_HEADS // 2):
        cols = slice(pair * LANE, (pair + 1) * LANE)
        prev = state[pair]
        prev_b = prev.astype(BF16)
        ys, news = [], []
        for h in (2 * pair, 2 * pair + 1):
            g = h // SSD_HPG
            dec = jnp.exp(jnp.where(tril, ac_w[:, h * LANE:(h + 1) * LANE] - a_cum_t[h:h + 1, :], -jnp.inf))
            y = _dot((cbms[g] * dec).astype(BF16), xdt_b[:, cols])
            ys.append(y + _dot(cm_gs[g], prev_b) * eac[:, cols])
            news.append(_dot(bmT_gs[g], w_b[:, cols]))
        state[pair] = sdec[:, cols] * prev + jnp.where(first_s, news[0], news[1])
        ybuf[:, cols] = jnp.where(first_y, ys[0], ys[1])

    z = zx_ref[:, 0:SSD_INNER].astype(F32)
    y = ybuf[...] + dsk_ref[...] * xs
    o_ref[...] = _rms(y * jax.nn.silu(z), ng_ref[...])


def _ssd(proj, bsz, s, cw, cb, dtb, alog, dsk, ng):
    t = bsz * s
    L = SSD_CHUNK
    R = SSD_PER_STEP * L
    nc = s // R
    const = lambda b, c: (0, 0)
    return pl.pallas_call(
        _ssd_kernel,
        grid=(bsz, nc),
        in_specs=[pl.BlockSpec((R, 2048), lambda b, c: (b * nc + c, 0)),
                  pl.BlockSpec((R, LANE), lambda b, c: (b * nc + c, P_DT // LANE)),
                  pl.BlockSpec((SSD_CONV, SSD_CONV_CH), const),
                  pl.BlockSpec((1, SSD_CONV_CH), const),
                  pl.BlockSpec((1, LANE), const),
                  pl.BlockSpec((1, LANE), const),
                  pl.BlockSpec((1, SSD_INNER), const),
                  pl.BlockSpec((1, SSD_INNER), const)],
        out_specs=pl.BlockSpec((R, SSD_INNER), lambda b, c: (b * nc + c, 0)),
        out_shape=jax.ShapeDtypeStruct((t, SSD_INNER), F32),
        scratch_shapes=[pltpu.VMEM((L + 8, SSD_CONV_CH), F32),
                        pltpu.VMEM((SSD_HEADS // 2, SSD_STATE, 2 * SSD_HEAD_DIM), F32),
                        pltpu.VMEM((L, SSD_INNER), F32)],
        compiler_params=_cparams("arbitrary", "arbitrary"),
        name="ssd",
    )(proj, proj, cw, cb, dtb, alog, dsk, ng)


def _s5_disc_kernel(lre_ref, lim_ref, ls_ref, bre_ref, bim_ref, ore_ref, oim_ref, obre_ref, obim_ref):
    lr = lre_ref[...]
    li = lim_ref[...]
    step = jnp.exp(ls_ref[...])
    mag = jnp.exp(lr * step)
    pr = mag * jnp.cos(li * step)
    pi = mag * jnp.sin(li * step)
    ore_ref[...] = pr
    oim_ref[...] = pi
    qr = pr - 1.0
    den = lr * lr + li * li
    cr = (qr * lr + pi * li) / den
    ci = (pi * lr - qr * li) / den
    br = bre_ref[...]
    bi = bim_ref[...]
    obre_ref[...] = cr * br - ci * bi
    obim_ref[...] = cr * bi + ci * br


def _s5_discretise(lam_re, lam_im, log_step, b_re, b_im):
    n = S5_STATE * S5_GROUP
    rep = lambda a: jnp.repeat(a, S5_GROUP, axis=-1)
    args = (rep(lam_re), rep(lam_im), jnp.broadcast_to(log_step[:, None], (S5_GROUPS, n)),
            b_re.reshape(S5_GROUPS, n), b_im.reshape(S5_GROUPS, n))
    outs = pl.pallas_call(
        _s5_disc_kernel,
        out_shape=[jax.ShapeDtypeStruct((S5_GROUPS, n), F32)] * 4,
        name="s5_disc",
    )(*args)
    lamb_re, lamb_im, bb_re, bb_im = outs
    lamb_re = lamb_re[:, ::S5_GROUP].reshape(1, S5_LANES)
    lamb_im = lamb_im[:, ::S5_GROUP].reshape(1, S5_LANES)
    return lamb_re, lamb_im, bb_re.reshape(S5_GROUPS, S5_STATE, S5_GROUP), bb_im.reshape(S5_GROUPS, S5_STATE, S5_GROUP)


def _block_diag(blocks):
    g, r, c = blocks.shape
    eye = jnp.eye(g, dtype=blocks.dtype)
    return (blocks[:, :, None, :] * eye[:, None, :, None]).reshape(g * r, g * c)


S5_COLS = 512
S5_BSPLIT = 2


def _s5_kernel(u_ref, lre_ref, lim_ref, bre_ref, bim_ref, cre_ref, cimn_ref, d_ref, wglu_ref, o_ref,
               hre, him, sre, sim, tbuf):
    nb, steps, _ = u_ref.shape
    rows = nb * steps
    nslab = S5_WIDTH // LANE

    @pl.when(pl.program_id(0) == 0)
    def _():
        sre[...] = jnp.zeros(sre.shape, F32)
        sim[...] = jnp.zeros(sim.shape, F32)

    for b in range(nb):
        for cs in range(nslab):
            tbuf[pl.ds(cs * rows + b, steps, stride=nb), :] = u_ref[b, :, cs * LANE:(cs + 1) * LANE].astype(F32)
    u = jnp.concatenate([tbuf[cs * rows:(cs + 1) * rows, :] for cs in range(nslab)], axis=1)
    ub = u.astype(BF16)
    kin = S5_WIDTH // S5_BSPLIT
    kst = S5_LANES // S5_BSPLIT
    for kb in range(S5_BSPLIT):
        hre[:, kb * kst:(kb + 1) * kst] = _dot(ub[:, kb * kin:(kb + 1) * kin],
                                               bre_ref[kb * kin:(kb + 1) * kin, kb * kst:(kb + 1) * kst])
        him[:, kb * kst:(kb + 1) * kst] = _dot(ub[:, kb * kin:(kb + 1) * kin],
                                               bim_ref[kb * kin:(kb + 1) * kin, kb * kst:(kb + 1) * kst])

    for cb in range(S5_LANES // S5_COLS):
        cols = slice(cb * S5_COLS, (cb + 1) * S5_COLS)
        lr = jnp.broadcast_to(lre_ref[:, cols], (8, S5_COLS))
        li = jnp.broadcast_to(lim_ref[:, cols], (8, S5_COLS))

        def body(t, carry, cols=cols, lr=lr, li=li):
            hr, hi = carry
            r0 = pl.multiple_of(t * 8, 8)
            nr = lr * hr - li * hi + hre[pl.ds(r0, 8), cols]
            ni = lr * hi + li * hr + him[pl.ds(r0, 8), cols]
            hre[pl.ds(r0, 8), cols] = nr
            him[pl.ds(r0, 8), cols] = ni
            return nr, ni

        hr, hi = lax.fori_loop(0, steps, body, (sre[:, cols], sim[:, cols]), unroll=8)
        sre[:, cols] = hr
        sim[:, cols] = hi

    kst = S5_LANES // nslab
    ys = []
    for cs in range(nslab):
        st = slice(cs * kst, (cs + 1) * kst)
        lanes = slice(cs * LANE, (cs + 1) * LANE)
        ys.append(_dot(hre[:, st].astype(BF16), cre_ref[st, lanes]) + _dot(him[:, st].astype(BF16), cimn_ref[st, lanes]))
    y = jnp.concatenate(ys, axis=1) + d_ref[...] * u
    y = jax.nn.gelu(y)
    y = y * jax.nn.sigmoid(_dot(y.astype(BF16), wglu_ref[...]))
    for cs in range(nslab):
        tbuf[cs * rows:(cs + 1) * rows, :] = y[:, cs * LANE:(cs + 1) * LANE]
    for b in range(nb):
        for cs in range(nslab):
            o_ref[b, :, cs * LANE:(cs + 1) * LANE] = tbuf[pl.ds(cs * rows + b, steps, stride=nb), :]


S5_STEPS = 64


def _s5(proj3, lamb_re, lamb_im, bmat_re, bmat_im, cmat_re, cmat_imn, d, wglu):
    bsz, s, _ = proj3.shape
    steps = min(S5_STEPS, s)
    rows = bsz * steps
    const = lambda i: (0, 0)
    return pl.pallas_call(
        _s5_kernel,
        grid=(s // steps,),
        in_specs=[pl.BlockSpec((bsz, steps, S5_WIDTH), lambda i: (0, i, P_S5 // S5_WIDTH)),
                  pl.BlockSpec((1, S5_LANES), const),
                  pl.BlockSpec((1, S5_LANES), const),
                  pl.BlockSpec((S5_WIDTH, S5_LANES), const),
                  pl.BlockSpec((S5_WIDTH, S5_LANES), const),
                  pl.BlockSpec((S5_LANES, S5_WIDTH), const),
                  pl.BlockSpec((S5_LANES, S5_WIDTH), const),
                  pl.BlockSpec((1, S5_WIDTH), const),
                  pl.BlockSpec((S5_WIDTH, S5_WIDTH), const)],
        out_specs=pl.BlockSpec((bsz, steps, S5_WIDTH), lambda i: (0, i, 0)),
        out_shape=jax.ShapeDtypeStruct((bsz, s, S5_WIDTH), F32),
        scratch_shapes=[pltpu.VMEM((rows, S5_LANES), F32), pltpu.VMEM((rows, S5_LANES), F32),
                        pltpu.VMEM((8, S5_LANES), F32), pltpu.VMEM((8, S5_LANES), F32),
                        pltpu.VMEM((S5_WIDTH // LANE * rows, LANE), F32)],
        compiler_params=_cparams("arbitrary"),
        name="s5",
    )(proj3, lamb_re, lamb_im, bmat_re, bmat_im, cmat_re, cmat_imn, d, wglu)


CONV_ROWS = 64


def _conv_kernel(scb_ref, scc_ref, sch_ref, cfa_ref, cfg_ref, scw_ref, cfw_ref, lng_ref, lnb_ref, oc_ref, od_ref,
                 chbuf, vbuf, ycbuf, ydbuf):
    nb, steps, _ = scb_ref.shape
    rows = nb * steps
    nslab = SC_WIDTH // LANE
    sc_halo = (SC_CONV - 1) * nb
    cf_halo = (CF_CONV - 1) * nb
    sc_pitch = sc_halo + rows
    cf_pitch = cf_halo + rows

    @pl.when(pl.program_id(0) == 0)
    def _():
        for cs in range(nslab):
            chbuf[cs * sc_pitch:cs * sc_pitch + sc_halo, :] = jnp.zeros((sc_halo, LANE), F32)
            vbuf[cs * cf_pitch:cs * cf_pitch + cf_halo, :] = jnp.zeros((cf_halo, LANE), F32)

    for b in range(nb):
        ch = scc_ref[b].astype(F32) * sch_ref[b].astype(F32)
        v = cfa_ref[b].astype(F32) * jax.nn.sigmoid(cfg_ref[b].astype(F32))
        for cs in range(nslab):
            lanes = slice(cs * LANE, (cs + 1) * LANE)
            chbuf[pl.ds(cs * sc_pitch + sc_halo + b, steps, stride=nb), :] = ch[:, lanes]
            vbuf[pl.ds(cs * cf_pitch + cf_halo + b, steps, stride=nb), :] = v[:, lanes]

    def block(rb, carry):
        r0 = pl.multiple_of(rb * CONV_ROWS, CONV_ROWS)
        for cs in range(nslab):
            lanes = slice(cs * LANE, (cs + 1) * LANE)
            acc = scw_ref[0:1, lanes] * chbuf[pl.ds(cs * sc_pitch + r0, CONV_ROWS), :]
            for k in range(1, SC_CONV):
                acc = acc + scw_ref[k:k + 1, lanes] * chbuf[pl.ds(cs * sc_pitch + k * nb + r0, CONV_ROWS), :]
            ycbuf[pl.ds(cs * rows + r0, CONV_ROWS), :] = acc
            acc = cfw_ref[0:1, lanes] * vbuf[pl.ds(cs * cf_pitch + r0, CONV_ROWS), :]
            for k in range(1, CF_CONV):
                acc = acc + cfw_ref[k:k + 1, lanes] * vbuf[pl.ds(cs * cf_pitch + k * nb + r0, CONV_ROWS), :]
            ydbuf[pl.ds(cs * rows + r0, CONV_ROWS), :] = acc
        return carry

    lax.fori_loop(0, rows // CONV_ROWS, block, 0)

    y = jnp.concatenate([ydbuf[cs * rows:(cs + 1) * rows, :] for cs in range(nslab)], axis=1)
    mu = jnp.mean(y, axis=-1, keepdims=True)
    var = jnp.mean(jnp.square(y - mu), axis=-1, keepdims=True)
    y = jax.nn.silu((y - mu) * lax.rsqrt(var + EPS) * lng_ref[...] + lnb_ref[...])
    for cs in range(nslab):
        ydbuf[cs * rows:(cs + 1) * rows, :] = y[:, cs * LANE:(cs + 1) * LANE]

    for b in range(nb):
        for cs in range(nslab):
            lanes = slice(cs * LANE, (cs + 1) * LANE)
            oc_ref[b, :, lanes] = scb_ref[b, :, lanes].astype(F32) * ycbuf[pl.ds(cs * rows + b, steps, stride=nb), :]
            od_ref[b, :, lanes] = ydbuf[pl.ds(cs * rows + b, steps, stride=nb), :]

    for cs in range(nslab):
        chbuf[cs * sc_pitch:cs * sc_pitch + sc_halo, :] = chbuf[cs * sc_pitch + rows:cs * sc_pitch + rows + sc_halo, :]
        vbuf[cs * cf_pitch:cs * cf_pitch + cf_halo, :] = vbuf[cs * cf_pitch + rows:cs * cf_pitch + rows + cf_halo, :]


CONV_STEPS = 64


def _convs(proj3, scw, cfw, lng, lnb):
    bsz, s, _ = proj3.shape
    steps = min(CONV_STEPS, s)
    rows = bsz * steps
    nslab = SC_WIDTH // LANE
    const = lambda i: (0, 0)
    blk = lambda col: pl.BlockSpec((bsz, steps, SC_WIDTH), lambda i: (0, i, col))
    out = pl.BlockSpec((bsz, steps, SC_WIDTH), lambda i: (0, i, 0))
    return pl.pallas_call(
        _conv_kernel,
        grid=(s // steps,),
        in_specs=[blk(P_SC // SC_WIDTH), blk(P_SC // SC_WIDTH + 1), blk(P_SC // SC_WIDTH + 2),
                  blk(P_CF // CF_WIDTH), blk(P_CF // CF_WIDTH + 1),
                  pl.BlockSpec((SC_CONV, SC_WIDTH), const),
                  pl.BlockSpec((CF_CONV, CF_WIDTH), const),
                  pl.BlockSpec((1, CF_WIDTH), const),
                  pl.BlockSpec((1, CF_WIDTH), const)],
        out_specs=[out, out],
        out_shape=[jax.ShapeDtypeStruct((bsz, s, SC_WIDTH), F32), jax.ShapeDtypeStruct((bsz, s, CF_WIDTH), F32)],
        scratch_shapes=[pltpu.VMEM((nslab * ((SC_CONV - 1) * bsz + rows), LANE), F32),
                        pltpu.VMEM((nslab * ((CF_CONV - 1) * bsz + rows), LANE), F32),
                        pltpu.VMEM((nslab * rows, LANE), F32),
                        pltpu.VMEM((nslab * rows, LANE), F32)],
        compiler_params=_cparams("arbitrary"),
        name="convs",
    )(proj3, proj3, proj3, proj3, proj3, scw, cfw, lng, lnb)


def _merge_kernel(x_ref, ya_ref, yb_ref, yc_ref, yd_ref, ga_ref, gb_ref, gc_ref, gd_ref, wb_ref, wout_ref, g2_ref,
                  wq_ref, xo_ref, h2_ref, q_ref):
    merged = None
    branches = ((ya_ref, ga_ref), (yb_ref, gb_ref), (yc_ref, gc_ref), (yd_ref, gd_ref))
    for i, (y_ref, gate_ref) in enumerate(branches):
        lo, hi = BRANCH_OFFSETS[i], BRANCH_OFFSETS[i + 1]
        t = _dot(y_ref[...].astype(BF16), wb_ref[lo:hi, :])
        t = jax.nn.sigmoid(gate_ref[...].astype(F32)) * t
        merged = t if merged is None else merged + t
    xn = x_ref[...] + _dot(merged.astype(BF16), wout_ref[...])
    xo_ref[...] = xn
    h2 = _rms(xn, g2_ref[...]).astype(BF16)
    h2_ref[...] = h2
    q_ref[...] = _dot(h2, wq_ref[...]).astype(BF16)


def _merge(x, ya, yb, yc, yd, proj, wb, wout, g2, wq):
    t = x.shape[0]
    tm = min(512, t)
    nq = wq.shape[1]
    row = lambda i: (i, 0)
    const = lambda i: (0, 0)
    return pl.pallas_call(
        _merge_kernel,
        grid=(t // tm,),
        in_specs=[pl.BlockSpec((tm, D_MODEL), row),
                  pl.BlockSpec((tm, SSD_INNER), row),
                  pl.BlockSpec((tm, S5_WIDTH), row),
                  pl.BlockSpec((tm, SC_WIDTH), row),
                  pl.BlockSpec((tm, CF_WIDTH), row),
                  pl.BlockSpec((tm, D_MODEL), lambda i: (i, P_GATE // D_MODEL)),
                  pl.BlockSpec((tm, D_MODEL), lambda i: (i, P_GATE // D_MODEL + 1)),
                  pl.BlockSpec((tm, D_MODEL), lambda i: (i, P_GATE // D_MODEL + 2)),
                  pl.BlockSpec((tm, D_MODEL), lambda i: (i, P_GATE // D_MODEL + 3)),
                  pl.BlockSpec((BRANCH_OFFSETS[-1], D_MODEL), const),
                  pl.BlockSpec((D_MODEL, D_MODEL), const),
                  pl.BlockSpec((1, D_MODEL), const),
                  pl.BlockSpec((D_MODEL, nq), const)],
        out_specs=[pl.BlockSpec((tm, D_MODEL), row), pl.BlockSpec((tm, D_MODEL), row), pl.BlockSpec((tm, nq), row)],
        out_shape=[jax.ShapeDtypeStruct((t, D_MODEL), F32), jax.ShapeDtypeStruct((t, D_MODEL), BF16),
                   jax.ShapeDtypeStruct((t, nq), BF16)],
        compiler_params=_cparams("arbitrary"),
        name="merge",
    )(x, ya, yb, yc, yd, proj, proj, proj, proj, wb, wout, g2, wq)


NEG_INF = float("-inf")
CAND_B0, CAND_B1, CAND_A0, CAND_A1, CAND_ROWS = 0, 16, 24, 40, 72
CAND_LAST_A = 4


def _top16(x, vals_ref, idx_ref, ids):
    big = jnp.float32(1e9)
    for r in range(PEER_TOPK):
        m = jnp.max(x, axis=0, keepdims=True)
        sel = jnp.min(jnp.where(x == m, ids, big), axis=0, keepdims=True)
        vals_ref[r:r + 1, :] = m
        idx_ref[r:r + 1, :] = sel
        x = jnp.where(ids == sel, NEG_INF, x)


def _route_kernel(q_ref, k1_ref, k2_ref, oi_ref, oj_ref, og_ref,
                  v1, i1, v2, i2, cand, best, pos, si, sj, sg):
    tn = q_ref.shape[0]
    key_ids = lax.broadcasted_iota(jnp.int32, (PEER_KEYS, tn), 0).astype(F32)
    r = lax.broadcasted_iota(jnp.int32, (CAND_ROWS, tn), 0)
    ra = r - CAND_A1
    b_in_a = jnp.where(r < CAND_A1, r - CAND_A0, ra & 7)
    pos_ids = jnp.where(r < CAND_B1, r * PEER_TOPK,
                        jnp.where(r < CAND_A0, (r - CAND_B1) * PEER_TOPK + 1,
                                  jnp.where(r < CAND_A1, b_in_a, (1 + (ra >> 3)) * PEER_TOPK + b_in_a))).astype(F32)
    cand_dup = jnp.logical_and(r >= CAND_A0, b_in_a < 2)

    def head(h, carry):
        c0 = pl.multiple_of(h * 2 * PEER_HALF, 2 * PEER_HALF)
        q1 = q_ref[:, pl.ds(c0, PEER_HALF)]
        q2 = q_ref[:, pl.ds(c0 + PEER_HALF, PEER_HALF)]
        _top16(_dot_nt(k1_ref[...], q1), v1, i1, key_ids)
        _top16(_dot_nt(k2_ref[...], q2), v2, i2, key_ids)
        v1a = v1[...]
        v2a = v2[...]
        cand[CAND_B0:CAND_B1, :] = v1a + v2a[0:1, :]
        cand[CAND_B1:CAND_A0, :] = v1a[0:8, :] + v2a[1:2, :]
        cand[CAND_A0:CAND_A1, :] = v1a[0:1, :] + v2a
        for a in range(1, CAND_LAST_A + 1):
            o = CAND_A1 + (a - 1) * 8
            cand[o:o + 8, :] = v1a[a:a + 1, :] + v2a[0:8, :]
        _top16(jnp.where(cand_dup, NEG_INF, cand[...]), best, pos, pos_ids)
        p = pos[...]
        a_k = jnp.floor(p * (1.0 / PEER_TOPK))
        b_k = p - a_k * PEER_TOPK
        isel = jnp.zeros((PEER_TOPK, tn), F32)
        jsel = jnp.zeros((PEER_TOPK, tn), F32)
        for a in range(PEER_TOPK):
            isel = jnp.where(a_k == a, i1[a:a + 1, :], isel)
            jsel = jnp.where(b_k == a, i2[a:a + 1, :], jsel)
        r0 = pl.multiple_of(h * PEER_TOPK, PEER_TOPK)
        si[pl.ds(r0, PEER_TOPK), :] = isel
        sj[pl.ds(r0, PEER_TOPK), :] = jsel
        sg[pl.ds(r0, PEER_TOPK), :] = jax.nn.softmax(best[...], axis=0)
        return carry

    lax.fori_loop(0, PEER_HEADS, head, 0)
    oi_ref[...] = si[...].T
    oj_ref[...] = sj[...].T
    og_ref[...] = sg[...].T


def _route(q, k1, k2):
    t = q.shape[0]
    tn = min(1024, t)
    row = lambda i: (i, 0)
    const = lambda i: (0, 0)
    vm = lambda n: pltpu.VMEM((n, tn), F32)
    return pl.pallas_call(
        _route_kernel,
        grid=(t // tn,),
        in_specs=[pl.BlockSpec((tn, q.shape[1]), row),
                  pl.BlockSpec((PEER_KEYS, PEER_HALF), const),
                  pl.BlockSpec((PEER_KEYS, PEER_HALF), const)],
        out_specs=[pl.BlockSpec((tn, PEER_SEL), row)] * 3,
        out_shape=[jax.ShapeDtypeStruct((t, PEER_SEL), F32)] * 3,
        scratch_shapes=[vm(PEER_TOPK), vm(PEER_TOPK), vm(PEER_TOPK), vm(PEER_TOPK), vm(CAND_ROWS),
                        vm(PEER_TOPK), vm(PEER_TOPK), vm(PEER_SEL), vm(PEER_SEL), vm(PEER_SEL)],
        compiler_params=_cparams("arbitrary"),
        name="route",
    )(q, k1, k2)


PEER_TE = 2048
GS_PITCH = 136
GS_TOK = 32


def _peer_kernel(h_ref, ii_ref, jj_ref, gg_ref, x_ref, u_ref, v_ref, fg_ref, o_ref, gmat, acc, gs, *, final):
    tm = h_ref.shape[0]
    e = pl.program_id(1)

    @pl.when(e == 0)
    def _():
        acc[...] = jnp.zeros(acc.shape, F32)
        ids = lax.broadcasted_iota(jnp.int32, (PEER_KEYS, PEER_SEL), 0).astype(F32)

        def group(gi, carry):
            t0 = pl.multiple_of(gi * GS_TOK, GS_TOK)
            for tt in range(GS_TOK):
                irow = ii_ref[pl.ds(t0 + tt, 1), :]
                jrow = jj_ref[pl.ds(t0 + tt, 1), :]
                grow = gg_ref[pl.ds(t0 + tt, 1), :]
                pt = jnp.where(ids == irow, 1.0, 0.0).astype(BF16)
                qt = jnp.where(ids == jrow, grow, 0.0).astype(BF16)
                gs[tt * GS_PITCH:tt * GS_PITCH + PEER_KEYS, :] = _dot_nt(pt, qt)

            for i in range(PEER_KEYS):
                blk = gs[pl.ds(i, GS_TOK, stride=GS_PITCH), :]
                gmat[pl.ds(t0, GS_TOK), i * PEER_KEYS:(i + 1) * PEER_KEYS] = blk.astype(BF16)
            return carry

        lax.fori_loop(0, tm // GS_TOK, group, 0)

    s = _dot_nt(h_ref[...], u_ref[...])
    w = gmat[:, pl.ds(pl.multiple_of(e * PEER_TE, PEER_TE), PEER_TE)]
    a = (jax.nn.gelu(s) * w.astype(F32)).astype(BF16)
    acc[...] += _dot(a, v_ref[...])

    @pl.when(e == pl.num_programs(1) - 1)
    def _():
        xn = x_ref[...] + acc[...]
        o_ref[...] = _rms(xn, fg_ref[...]) if final else xn


def _peer(h2, ii, jj, gg, x, u, v, l, fg, final):
    t = h2.shape[0]
    tm = min(512, t)
    row = lambda i, e: (i, 0)
    return pl.pallas_call(
        functools.partial(_peer_kernel, final=final),
        grid=(t // tm, N_EXPERTS // PEER_TE),
        in_specs=[pl.BlockSpec((tm, D_MODEL), row),
                  pl.BlockSpec((tm, PEER_SEL), row),
                  pl.BlockSpec((tm, PEER_SEL), row),
                  pl.BlockSpec((tm, PEER_SEL), row),
                  pl.BlockSpec((tm, D_MODEL), row),
                  pl.BlockSpec((None, PEER_TE, D_MODEL), lambda i, e: (l, e, 0)),
                  pl.BlockSpec((None, PEER_TE, D_MODEL), lambda i, e: (l, e, 0)),
                  pl.BlockSpec((1, D_MODEL), lambda i, e: (0, 0))],
        out_specs=pl.BlockSpec((tm, D_MODEL), row),
        out_shape=jax.ShapeDtypeStruct((t, D_MODEL), F32),
        scratch_shapes=[pltpu.VMEM((tm, N_EXPERTS), BF16),
                        pltpu.VMEM((tm, D_MODEL), F32),
                        pltpu.VMEM((GS_TOK * GS_PITCH, PEER_KEYS), F32)],
        compiler_params=_cparams("arbitrary", "arbitrary"),
        name="peer",
    )(h2, ii, jj, gg, x, u, v, fg)


def _pad_lanes(v, n=LANE):
    return jnp.pad(v, (0, n - v.shape[0])).reshape(1, n)


N_ZX_TILES = OFF_DT // PROJ_TN
DT_TILE = P_DT // PROJ_TN


def _relayout_kernel(a_ref, b_ref, o_ref):
    j = pl.program_id(0)

    @pl.when(j < N_ZX_TILES)
    def _():
        o_ref[...] = a_ref[...].astype(BF16)

    @pl.when(jnp.logical_and(j >= N_ZX_TILES, j < DT_TILE))
    def _():
        w = jnp.concatenate([a_ref[...], b_ref[...]], axis=1)
        o_ref[...] = w[:, SSD_HEADS:SSD_HEADS + PROJ_TN].astype(BF16)

    @pl.when(j == DT_TILE)
    def _():
        lane = lax.broadcasted_iota(jnp.int32, a_ref.shape, 1)
        o_ref[...] = jnp.where(lane < SSD_HEADS, a_ref[...], 0.0).astype(BF16)


def _relayout_w_in(w, l):
    assert OFF_DT == P_S5 and IN_COLS - OFF_S5 == P_DT - P_S5 and OFF_DT % PROJ_TN == 0
    a_idx = lambda j: (l, 0, jnp.where(j == DT_TILE, N_ZX_TILES, j))
    b_idx = lambda j: (l, 0, jnp.clip(j + 1, N_ZX_TILES + 1, DT_TILE) * (PROJ_TN // LANE))
    return pl.pallas_call(
        _relayout_kernel,
        grid=(DT_TILE + 1,),
        in_specs=[pl.BlockSpec((None, D_MODEL, PROJ_TN), a_idx), pl.BlockSpec((None, D_MODEL, LANE), b_idx)],
        out_specs=pl.BlockSpec((D_MODEL, PROJ_TN), lambda j: (0, j)),
        out_shape=jax.ShapeDtypeStruct((D_MODEL, P_COLS), BF16),
        compiler_params=_cparams("arbitrary"),
        name="w_in_relayout",
    )(w, w)


def kernel(x, norm1_g, w_in, ssd_conv_w, ssd_conv_b, ssd_dt_bias, ssd_a_log, ssd_d, ssd_norm_g, s5_lam_re, s5_lam_im, s5_log_step, s5_b_re, s5_b_im, s5_c_re, s5_c_im, s5_d, s5_w_glu, sc_conv_w, cf_conv_w, cf_ln_g, cf_ln_b, w_branch, w_out, norm2_g, peer_w_query, peer_sub_keys, peer_u, peer_v, final_norm_g):
    bsz, s, d = x.shape
    t = bsz * s
    depth = w_in.shape[0]
    xt = x.reshape(t, d)
    u_b = peer_u.astype(BF16)
    v_b = peer_v.astype(BF16)
    for l in range(depth):
        proj = _inproj(xt, norm1_g[l].reshape(1, d), _relayout_w_in(w_in, l))

        ya = _ssd(proj, bsz, s, ssd_conv_w[l], ssd_conv_b[l].reshape(1, -1), _pad_lanes(ssd_dt_bias[l]),
                  _pad_lanes(ssd_a_log[l]), jnp.repeat(ssd_d[l], SSD_HEAD_DIM).reshape(1, -1),
                  ssd_norm_g[l].reshape(1, -1))

        lamb_re, lamb_im, bb_re, bb_im = _s5_discretise(s5_lam_re[l], s5_lam_im[l], s5_log_step[l],
                                                        s5_b_re[l], s5_b_im[l])
        bmat_re = _block_diag(bb_re.transpose(0, 2, 1)).astype(BF16)
        bmat_im = _block_diag(bb_im.transpose(0, 2, 1)).astype(BF16)
        cmat_re = _block_diag(s5_c_re[l].transpose(0, 2, 1)).astype(BF16)
        cmat_imn = _block_diag(-s5_c_im[l].transpose(0, 2, 1)).astype(BF16)
        proj3 = proj.reshape(bsz, s, P_COLS)
        yb = _s5(proj3, lamb_re, lamb_im, bmat_re, bmat_im, cmat_re, cmat_imn,
                 s5_d[l].reshape(1, -1), s5_w_glu[l].astype(BF16)).reshape(t, S5_WIDTH)

        yc, yd = _convs(proj3, sc_conv_w[l], cf_conv_w[l], cf_ln_g[l].reshape(1, -1), cf_ln_b[l].reshape(1, -1))
        yc = yc.reshape(t, SC_WIDTH)
        yd = yd.reshape(t, CF_WIDTH)

        xt, h2, q = _merge(xt, ya, yb, yc, yd, proj, w_branch[l].astype(BF16), w_out[l].astype(BF16),
                           norm2_g[l].reshape(1, d), peer_w_query[l].astype(BF16))

        ii, jj, gg = _route(q, peer_sub_keys[l, 0].astype(BF16), peer_sub_keys[l, 1].astype(BF16))
        xt = _peer(h2, ii, jj, gg, xt, u_b, v_b, l, final_norm_g.reshape(1, d), final=(l == depth - 1))
    return xt.reshape(bsz, s, d)
```
